```python
import jax, jax.numpy as jnp
from jax import lax
import numpy as np

D_MODEL = 2048
BATCH = 16
SEQ = 256
DEPTH = 2
DEC_BATCH = 8
DEC_SEQ = 1024
PAST_LEN = 512

GRID_W = 64
QBLOCK = 128
ROPE_BASE = 10000.0
RMS_EPS = 1e-6
LN_EPS = 1e-5
NEG_INF = -1e30

N_EVEN = (DEPTH + 1) // 2
N_ODD = DEPTH // 2

MLA_HEADS = 8
Q_LORA = 512
KV_LORA = 256
QK_NOPE = 128
QK_ROPE = 64
V_HEAD = 128
MLA_OUT = MLA_HEADS * V_HEAD
POOL_WINDOWS = (2, 4, 8, 16)
POOL_GROUPS = len(POOL_WINDOWS)
POOL_CH = D_MODEL // 2
POOL_GC = POOL_CH // POOL_GROUPS
EVEN_IN = Q_LORA + KV_LORA + QK_ROPE + POOL_CH
EVEN_MIX = MLA_OUT + POOL_CH
SWA_HEADS = 32
SWA_KV = 4
SWA_HD = 64
SWA_GROUP = SWA_HEADS // SWA_KV
WINDOW = 128
ODD_MIX = SWA_HEADS * SWA_HD
ODD_IN = ODD_MIX + 2 * SWA_KV * SWA_HD
N_EXPERTS = 16
EXPERT_FF = 2048
CAP_FACTOR = 2
DN_ALPHA = (2 * DEPTH) ** 0.25
DN_BETA = (8 * DEPTH) ** -0.25

kernel_name = 'hybrid_diffusion_mla_pool_swa_ecmoe_step'

f32 = jnp.float32


def rms_norm(x, g):
    xf = x.astype(f32)
    y = xf * lax.rsqrt(jnp.mean(xf * xf, -1, keepdims=True) + RMS_EPS)
    return (y * g.astype(f32)).astype(x.dtype)


def layer_norm(x, g, b):
    xf = x.astype(f32)
    xc = xf - jnp.mean(xf, -1, keepdims=True)
    var = jnp.mean(xc * xc, -1, keepdims=True)
    return (xc * lax.rsqrt(var + LN_EPS) * g.astype(f32) + b.astype(f32)).astype(x.dtype)


def axial_rope_angles(n_tokens, rot_dim):
    rows = n_tokens // GRID_W
    row = jnp.repeat(jnp.arange(rows, dtype=f32), GRID_W)
    col = jnp.tile(jnp.arange(GRID_W, dtype=f32), rows)
    n_freq = rot_dim // 4
    inv = ROPE_BASE ** (-jnp.arange(n_freq, dtype=f32) / n_freq)
    ang = jnp.concatenate([row[:, None] * inv, col[:, None] * inv], -1)
    return jnp.cos(ang), jnp.sin(ang)


def apply_rope(x, cos, sin):
    shape = (1, x.shape[1]) + (1,) * (x.ndim - 3) + (cos.shape[-1],)
    c, s = cos.reshape(shape), sin.reshape(shape)
    xf = x.astype(f32)
    x1, x2 = xf[..., 0::2], xf[..., 1::2]
    out = jnp.stack([x1 * c - x2 * s, x1 * s + x2 * c], -1).reshape(x.shape)
    return out.astype(x.dtype)


def dense_attention(q, k, v, sink=None):
    b, sq, hk, g, dk = q.shape
    scale = dk ** -0.5
    nb = sq // QBLOCK
    qb = jnp.moveaxis(q.reshape(b, nb, QBLOCK, hk, g, dk), 1, 0)

    def block(qi):
        s = jnp.einsum('bqhgd,bkhd->bhgqk', qi, k, preferred_element_type=f32) * scale
        if sink is None:
            p = jax.nn.softmax(s, -1)
        else:
            s_sink = jnp.broadcast_to(sink.astype(f32)[None, :, :, None, None], s.shape[:-1] + (1,))
            p = jax.nn.softmax(jnp.concatenate([s, s_sink], -1), -1)[..., :-1]
        return jnp.einsum('bhgqk,bkhe->bqhge', p.astype(v.dtype), v)

    out = lax.map(block, qb)
    return jnp.moveaxis(out, 0, 1).reshape(b, sq, hk, g, v.shape[-1])


def window_attention(q, k, v, k_ctx, v_ctx, sink):
    b, s_len, hk, g, d = q.shape
    nb = s_len // QBLOCK
    n_ctx = k_ctx.shape[1]
    scale = d ** -0.5
    pad = ((0, 0), (QBLOCK, QBLOCK), (0, 0), (0, 0))

    def band(t):
        tb = jnp.pad(t, pad).reshape(b, nb + 2, QBLOCK, hk, t.shape[-1])
        return jnp.moveaxis(jnp.concatenate([tb[:, :-2], tb[:, 1:-1], tb[:, 2:]], 2), 1, 0)

    kb, vb = band(k), band(v)
    qb = jnp.moveaxis(q.reshape(b, nb, QBLOCK, hk, g, d), 1, 0)
    k_off = jnp.arange(3 * QBLOCK) - QBLOCK
    rel = k_off[None, :] - jnp.arange(QBLOCK)[:, None]
    in_window = jnp.abs(rel) <= WINDOW
    sink_f = sink.astype(f32)

    def block(args):
        i, qi, ki, vi = args
        kpos = i * QBLOCK + k_off
        valid = in_window & ((kpos >= 0) & (kpos < s_len))[None, :]
        s_loc = jnp.einsum('bqhgd,bkhd->bhgqk', qi, ki, preferred_element_type=f32) * scale
        s_loc = jnp.where(valid, s_loc, NEG_INF)
        s_ctx = jnp.einsum('bqhgd,bkhd->bhgqk', qi, k_ctx, preferred_element_type=f32) * scale
        s_sink = jnp.broadcast_to(sink_f[None, :, :, None, None], s_loc.shape[:-1] + (1,))
        p = jax.nn.softmax(jnp.concatenate([s_loc, s_ctx, s_sink], -1), -1)
        p_loc = p[..., :3 * QBLOCK].astype(vi.dtype)
        p_ctx = p[..., 3 * QBLOCK:3 * QBLOCK + n_ctx].astype(v_ctx.dtype)
        return (jnp.einsum('bhgqk,bkhe->bqhge', p_loc, vi)
                + jnp.einsum('bhgqk,bkhe->bqhge', p_ctx, v_ctx))

    out = lax.map(block, (jnp.arange(nb), qb, kb, vb))
    return jnp.moveaxis(out, 0, 1).reshape(b, s_len, hk, g, d)


def pool_mixer(u, w_pool, pool_scale):
    b, s_len, _ = u.shape
    uf = u.astype(f32).reshape(b, s_len, POOL_GROUPS, POOL_GC)
    cs = jnp.pad(jnp.cumsum(uf, axis=1), ((0, 0), (1, 0), (0, 0), (0, 0)))
    t = jnp.arange(s_len)
    outs = []
    for gi, w in enumerate(POOL_WINDOWS):
        lo = jnp.clip(t - w // 2, 0, s_len)
        hi = jnp.clip(t + w // 2, 0, s_len)
        cnt = (hi - lo).astype(f32)[None, :, None]
        csg = cs[:, :, gi]
        outs.append((csg[:, hi] - csg[:, lo]) / cnt - uf[:, :, gi])
    pooled = jnp.stack(outs, 2).astype(u.dtype)
    mixed = jnp.einsum('bsgc,gce->bsge', pooled, w_pool)
    return mixed.reshape(b, s_len, POOL_CH) * pool_scale


def mla_keys(c_kv, k_pe, w_ukv):
    b, s_len, _ = c_kv.shape
    kv = (c_kv @ w_ukv).reshape(b, s_len, MLA_HEADS, QK_NOPE + V_HEAD)
    k_pe_h = jnp.broadcast_to(k_pe[:, :, None, :], (b, s_len, MLA_HEADS, QK_ROPE))
    return jnp.concatenate([kv[..., :QK_NOPE], k_pe_h], -1), kv[..., QK_NOPE:]


def even_mixer(h, e, P, ctx):
    b, s_len, _ = h.shape
    proj = h @ P['w_in_even'][e]
    c_q, c_kv, k_pe, u_pool = jnp.split(proj, [Q_LORA, Q_LORA + KV_LORA, Q_LORA + KV_LORA + QK_ROPE], axis=-1)
    c_q = rms_norm(c_q, P['q_norm'][e])
    c_kv = rms_norm(c_kv, P['kv_norm'][e])
    q = (c_q @ P['w_uq'][e]).reshape(b, s_len, MLA_HEADS, QK_NOPE + QK_ROPE)
    w_ukv = P['w_ukv'][e]
    if ctx is None:
        k, v = mla_keys(c_kv, k_pe, w_ukv)
        new = (c_kv, k_pe)
    else:
        cos, sin = axial_rope_angles(s_len, QK_ROPE)
        q = jnp.concatenate([q[..., :QK_NOPE], apply_rope(q[..., QK_NOPE:], cos, sin)], -1)
        k_pe_rot = apply_rope(k_pe[:, :, None, :], cos, sin)[:, :, 0, :]
        k_lat, v_lat = mla_keys(c_kv, k_pe_rot, w_ukv)
        k_ctx, v_ctx = mla_keys(ctx[0], ctx[1], w_ukv)
        k = jnp.concatenate([k_lat, k_ctx], 1)
        v = jnp.concatenate([v_lat, v_ctx], 1)
        new = None
    attn = dense_attention(q[:, :, :, None, :], k, v)
    pooled = pool_mixer(u_pool, P['w_pool'][e], P['pool_scale'][e])
    mix = jnp.concatenate([attn.reshape(b, s_len, MLA_OUT), pooled], -1) @ P['w_out_even'][e]
    return mix, new


def odd_mixer(h, o, P, ctx):
    b, s_len, _ = h.shape
    proj = h @ P['w_in_odd'][o]
    q, k, v = jnp.split(proj, [ODD_MIX, ODD_MIX + SWA_KV * SWA_HD], axis=-1)
    q = q.reshape(b, s_len, SWA_KV, SWA_GROUP, SWA_HD)
    k = k.reshape(b, s_len, SWA_KV, SWA_HD)
    v = v.reshape(b, s_len, SWA_KV, SWA_HD)
    sink = P['sink'][o].reshape(SWA_KV, SWA_GROUP)
    if ctx is None:
        attn = dense_attention(q, k, v, sink)
        new = (k, v)
    else:
        cos, sin = axial_rope_angles(s_len, SWA_HD)
        attn = window_attention(apply_rope(q, cos, sin), apply_rope(k, cos, sin), v, ctx[0], ctx[1], sink)
        new = None
    return attn.reshape(b, s_len, ODD_MIX) @ P['w_out_odd'][o], new


def ec_moe(x, w_router, w_gate, w_up, w_down):
    b, n, _ = x.shape
    cap = max(1, CAP_FACTOR * n // N_EXPERTS)
    aff = jax.nn.softmax(jnp.einsum('bnd,de->bne', x, w_router, preferred_element_type=f32), -1)
    top_v, top_i = lax.top_k(jnp.swapaxes(aff, 1, 2), cap)
    b_idx = jnp.arange(b)[:, None, None]
    xs = x[b_idx, top_i]
    hid = jax.nn.silu(jnp.einsum('becd,edf->becf', xs, w_gate)) * jnp.einsum('becd,edf->becf', xs, w_up)
    y = jnp.einsum('becf,efd->becd', hid, w_down) * top_v[..., None].astype(x.dtype)
    return jnp.zeros_like(x).at[b_idx, top_i].add(y.astype(x.dtype))


def modulation(cond, w_ada, b_ada):
    m = (jax.nn.silu(cond) @ w_ada + b_ada)[:, None, :]
    return jnp.split(m, 6, axis=-1)


def run_trunk(x, cond, P, ctx_caches):
    ckv_l, kpe_l, k_l, v_l = [], [], [], []
    for l in range(DEPTH):
        sh1, sc1, g1, sh2, sc2, g2 = modulation(cond, P['w_ada'][l], P['b_ada'][l])
        h = x * (1 + sc1) + sh1
        if l % 2 == 0:
            e = l // 2
            ctx = None if ctx_caches is None else (ctx_caches[0][:, e], ctx_caches[1][:, e])
            mix, new = even_mixer(h, e, P, ctx)
            if new is not None:
                ckv_l.append(new[0])
                kpe_l.append(new[1])
        else:
            o = l // 2
            ctx = None if ctx_caches is None else (ctx_caches[2][:, o], ctx_caches[3][:, o])
            mix, new = odd_mixer(h, o, P, ctx)
            if new is not None:
                k_l.append(new[0])
                v_l.append(new[1])
        x = layer_norm(DN_ALPHA * x + g1 * mix, P['ln1_g'][l], P['ln1_b'][l])
        h = x * (1 + sc2) + sh2
        ffn = ec_moe(h, P['w_router'][l], P['w_gate'][l], P['w_up'][l], P['w_down'][l])
        x = layer_norm(DN_ALPHA * x + g2 * ffn, P['ln2_g'][l], P['ln2_b'][l])
    return x, (ckv_l, kpe_l, k_l, v_l)


def setup_inputs(seed: int = 0) -> dict:
    key = jax.random.key(seed)
    ks = iter(jax.random.split(key, 40))

    def nrm(shape, scale=1.0):
        return jax.random.normal(next(ks), shape, jnp.float32) * scale

    d = D_MODEL
    return {
        'x_prompt': nrm((BATCH, SEQ, d)),
        'x_sample': nrm((DEC_BATCH, DEC_SEQ, d)),
        'cache_mla_ckv': nrm((DEC_BATCH, N_EVEN, PAST_LEN, KV_LORA)),
        'cache_mla_kpe': nrm((DEC_BATCH, N_EVEN, PAST_LEN, QK_ROPE)),
        'cache_swa_k': nrm((DEC_BATCH, N_ODD, PAST_LEN, SWA_KV, SWA_HD)),
        'cache_swa_v': nrm((DEC_BATCH, N_ODD, PAST_LEN, SWA_KV, SWA_HD)),
        'c': nrm((DEC_BATCH, d)),
        'c_ctx': nrm((d,)),
        'w_ada': nrm((DEPTH, d, 6 * d), 0.5 * d ** -0.5),
        'b_ada': nrm((DEPTH, 6 * d), 0.02),
        'w_in_even': nrm((N_EVEN, d, EVEN_IN), d ** -0.5),
        'q_norm': 1.0 + nrm((N_EVEN, Q_LORA), 0.02),
        'kv_norm': 1.0 + nrm((N_EVEN, KV_LORA), 0.02),
        'w_uq': nrm((N_EVEN, Q_LORA, MLA_HEADS * (QK_NOPE + QK_ROPE)), Q_LORA ** -0.5),
        'w_ukv': nrm((N_EVEN, KV_LORA, MLA_HEADS * (QK_NOPE + V_HEAD)), KV_LORA ** -0.5),
        'w_pool': nrm((N_EVEN, POOL_GROUPS, POOL_GC, POOL_GC), POOL_GC ** -0.5),
        'pool_scale': 1.0 + nrm((N_EVEN, POOL_CH), 0.1),
        'w_out_even': nrm((N_EVEN, EVEN_MIX, d), DN_BETA * EVEN_MIX ** -0.5),
        'w_in_odd': nrm((N_ODD, d, ODD_IN), d ** -0.5),
        'sink': nrm((N_ODD, SWA_HEADS)),
        'w_out_odd': nrm((N_ODD, ODD_MIX, d), DN_BETA * ODD_MIX ** -0.5),
        'ln1_g': 1.0 + nrm((DEPTH, d), 0.02),
        'ln1_b': nrm((DEPTH, d), 0.02),
        'w_router': nrm((DEPTH, d, N_EXPERTS), d ** -0.5),
        'w_gate': nrm((DEPTH, N_EXPERTS, d, EXPERT_FF), d ** -0.5),
        'w_up': nrm((DEPTH, N_EXPERTS, d, EXPERT_FF), d ** -0.5),
        'w_down': nrm((DEPTH, N_EXPERTS, EXPERT_FF, d), DN_BETA * EXPERT_FF ** -0.5),
        'ln2_g': 1.0 + nrm((DEPTH, d), 0.02),
        'ln2_b': nrm((DEPTH, d), 0.02),
    }


def reference(x_prompt, x_sample, cache_mla_ckv, cache_mla_kpe, cache_swa_k, cache_swa_v, c, c_ctx,
              w_ada, b_ada, w_in_even, q_norm, kv_norm, w_uq, w_ukv, w_pool, pool_scale, w_out_even,
              w_in_odd, sink, w_out_odd, ln1_g, ln1_b, w_router, w_gate, w_up, w_down, ln2_g, ln2_b):
    P = {
        'w_ada': w_ada, 'b_ada': b_ada,
        'w_in_even': w_in_even, 'q_norm': q_norm, 'kv_norm': kv_norm, 'w_uq': w_uq, 'w_ukv': w_ukv,
        'w_pool': w_pool, 'pool_scale': pool_scale, 'w_out_even': w_out_even,
        'w_in_odd': w_in_odd, 'sink': sink, 'w_out_odd': w_out_odd,
        'ln1_g': ln1_g, 'ln1_b': ln1_b, 'ln2_g': ln2_g, 'ln2_b': ln2_b,
        'w_router': w_router, 'w_gate': w_gate, 'w_up': w_up, 'w_down': w_down,
    }
    y_prompt, (ckv_l, kpe_l, k_l, v_l) = run_trunk(x_prompt, c_ctx[None, :], P, None)
    new_mla_ckv = jnp.stack(ckv_l, axis=1)
    new_mla_kpe = jnp.stack(kpe_l, axis=1)
    new_swa_k = jnp.stack(k_l, axis=1)
    new_swa_v = jnp.stack(v_l, axis=1)
    y_sample, _ = run_trunk(x_sample, c, P, (cache_mla_ckv, cache_mla_kpe, cache_swa_k, cache_swa_v))
    return (y_prompt, y_sample, new_mla_ckv, new_mla_kpe, new_swa_k, new_swa_v)
```

```python
import functools
import math

import numpy as np
import jax
import jax.numpy as jnp
from jax import lax
from jax.experimental import pallas as pl
from jax.experimental.pallas import tpu as pltpu

f32 = jnp.float32
bf16 = jnp.bfloat16

MLA_HEADS = 8
QK_NOPE = 128
QK_ROPE = 64
V_HEAD = 128
POOL_WINDOWS = (2, 4, 8, 16)
SWA_KV = 4
SWA_HD = 64
WINDOW = 128
CAP_FACTOR = 2
GRID_W = 64
ROPE_BASE = 10000.0
RMS_EPS = 1e-6
LN_EPS = 1e-5

LANES = 128
ROW_TILE = 256
MIB = 1024 * 1024


def _cparams(sem, vmem_mib):
    return pltpu.CompilerParams(dimension_semantics=sem, vmem_limit_bytes=vmem_mib * MIB)


def _dot(a, b):
    return jnp.dot(a, b, preferred_element_type=f32)


def _dot_nt(a, b):
    return lax.dot_general(a, b, (((1,), (1,)), ((), ())), preferred_element_type=f32)


def _silu(x):
    return x / (1.0 + jnp.exp(-x))


def _rope(r, c, sa, sb):
    return r * c + pltpu.roll(r, LANES - 1, axis=1) * sa + pltpu.roll(r, 1, axis=1) * sb


def _layer_norm(z, g, b):
    mu = jnp.mean(z, axis=-1, keepdims=True)
    zc = z - mu
    var = jnp.mean(zc * zc, axis=-1, keepdims=True)
    return zc * lax.rsqrt(var + LN_EPS) * g + b


def _mod_kernel(c_ref, w_ref, b_ref, o_ref):
    a = _silu(c_ref[...]).astype(bf16)
    o_ref[0] = _dot(a, w_ref[0].astype(bf16)) + b_ref[0]


def _modulation(cond, w_ada, b_ada):
    depth, d, n = w_ada.shape
    r = cond.shape[0]
    tn = 1024
    return pl.pallas_call(
        _mod_kernel,
        out_shape=jax.ShapeDtypeStruct((depth, r, n), f32),
        grid=(depth, n // tn),
        in_specs=[pl.BlockSpec((r, d), lambda l, j: (0, 0)),
                  pl.BlockSpec((1, d, tn), lambda l, j: (l, 0, j)),
                  pl.BlockSpec((1, 1, tn), lambda l, j: (l, 0, j))],
        out_specs=pl.BlockSpec((1, r, tn), lambda l, j: (l, 0, j)),
        compiler_params=_cparams(("arbitrary", "arbitrary"), 40),
        name="adaln_modulation",
    )(cond, w_ada, b_ada.reshape(depth, 1, n))


def _in_even_kernel(x_ref, mod_ref, win_ref, qn_ref, kvn_ref, wuq_ref, wukv_ref, rc_ref, rsa_ref, rsb_ref,
                    q_ref, ckv_ref, kv_ref, kper_ref, kpea_ref, u_ref, *, rope, q_lora, kv_lora, qscale):
    sh = mod_ref[0, 0:1, :]
    sc = mod_ref[0, 1:2, :]
    h = (x_ref[...] * (1.0 + sc) + sh).astype(bf16)
    proj = _dot(h, win_ref[...])
    o_kv = q_lora
    o_pe = q_lora + kv_lora
    o_u = o_pe + LANES
    pq = proj[:, 0:o_kv]
    pkv = proj[:, o_kv:o_pe]
    kpe = proj[:, o_pe:o_u]
    u_ref[...] = proj[:, o_u:]
    cq = pq * lax.rsqrt(jnp.mean(pq * pq, axis=-1, keepdims=True) + RMS_EPS) * qn_ref[...]
    ckv = pkv * lax.rsqrt(jnp.mean(pkv * pkv, axis=-1, keepdims=True) + RMS_EPS) * kvn_ref[...]
    ckv_ref[...] = ckv
    kv_ref[...] = _dot(ckv.astype(bf16), wukv_ref[...]).astype(bf16)
    kper_ref[...] = kpe
    q = _dot(cq.astype(bf16), wuq_ref[...]) * qscale
    if rope:
        c, sa, sb = rc_ref[...], rsa_ref[...], rsb_ref[...]
        kpea_ref[...] = _rope(kpe, c, sa, sb).astype(bf16)
    else:
        kpea_ref[...] = kpe.astype(bf16)
    for hd in range(MLA_HEADS):
        a = 2 * hd * LANES
        q_ref[:, a:a + LANES] = q[:, a:a + LANES].astype(bf16)
        r = q[:, a + LANES:a + 2 * LANES]
        if rope:
            r = _rope(r, c, sa, sb)
        q_ref[:, a + LANES:a + 2 * LANES] = r.astype(bf16)


def _in_even(x, mods, w_in, qn, kvn, wuq, wukv, rtab, *, rope, tiles_per_mod, tiles_per_seq):
    t, d = x.shape
    q_lora, kv_lora = qn.shape[1], kvn.shape[1]
    n_in = w_in.shape[1]
    n_u = n_in - q_lora - kv_lora - LANES
    nq, nkv = wuq.shape[1], wukv.shape[1]
    tm = ROW_TILE
    row = lambda i: (i, 0)
    const = lambda i: (0, 0)
    rmap = lambda i: (i % tiles_per_seq, 0)
    kern = functools.partial(_in_even_kernel, rope=rope, q_lora=q_lora, kv_lora=kv_lora,
                             qscale=(QK_NOPE + QK_ROPE) ** -0.5)
    return pl.pallas_call(
        kern,
        out_shape=(jax.ShapeDtypeStruct((t, nq), bf16), jax.ShapeDtypeStruct((t, kv_lora), f32),
                   jax.ShapeDtypeStruct((t, nkv), bf16), jax.ShapeDtypeStruct((t, LANES), f32),
                   jax.ShapeDtypeStruct((t, LANES), bf16), jax.ShapeDtypeStruct((t, n_u), f32)),
        grid=(t // tm,),
        in_specs=[pl.BlockSpec((tm, d), row),
                  pl.BlockSpec((1, 6, d), lambda i: (i // tiles_per_mod, 0, 0)),
                  pl.BlockSpec((d, n_in), const),
                  pl.BlockSpec((1, q_lora), const), pl.BlockSpec((1, kv_lora), const),
                  pl.BlockSpec((q_lora, nq), const), pl.BlockSpec((kv_lora, nkv), const),
                  pl.BlockSpec((tm, LANES), rmap), pl.BlockSpec((tm, LANES), rmap), pl.BlockSpec((tm, LANES), rmap)],
        out_specs=(pl.BlockSpec((tm, nq), row), pl.BlockSpec((tm, kv_lora), row), pl.BlockSpec((tm, nkv), row),
                   pl.BlockSpec((tm, LANES), row), pl.BlockSpec((tm, LANES), row), pl.BlockSpec((tm, n_u), row)),
        compiler_params=_cparams(("arbitrary",), 48),
        name="in_proj_even",
    )(x, mods, w_in, qn, kvn, wuq, wukv, *rtab)


def _mm_kernel(x_ref, w_ref, o_ref):
    o_ref[...] = _dot(x_ref[...].astype(bf16), w_ref[...]).astype(o_ref.dtype)


def _matmul(x, w, out_dtype, tm=512):
    m, k = x.shape
    n = w.shape[1]
    tm = min(tm, m)
    assert m % tm == 0
    return pl.pallas_call(
        _mm_kernel,
        out_shape=jax.ShapeDtypeStruct((m, n), out_dtype),
        grid=(m // tm,),
        in_specs=[pl.BlockSpec((tm, k), lambda i: (i, 0)), pl.BlockSpec((k, n), lambda i: (0, 0))],
        out_specs=pl.BlockSpec((tm, n), lambda i: (i, 0)),
        compiler_params=_cparams(("arbitrary",), 24),
        name="ctx_kv_up",
    )(x, w)


def _pool_kernel(u_ref, wp_ref, ps_ref, o_ref, *, seq):
    gc = wp_ref.shape[1]
    t = lax.broadcasted_iota(jnp.int32, (seq, 1), 0)

    def down(a, k):
        return jnp.where(t >= k, pltpu.roll(a, k, axis=0), 0.0)

    def up(a, k):
        return jnp.where(t < seq - k, pltpu.roll(a, seq - k, axis=0), 0.0)

    for gi, w in enumerate(POOL_WINDOWS):
        half = w // 2
        a = u_ref[:, gi * gc:(gi + 1) * gc]
        trail, lead, k = a, a, 1
        while k < half:
            trail = trail + down(trail, k)
            lead = lead + up(lead, k)
            k *= 2
        win = down(trail, 1) + lead
        cnt = (jnp.minimum(t + half, seq) - jnp.maximum(t - half, 0)).astype(f32)
        pooled = win / cnt - a
        mixed = _dot(pooled.astype(bf16), wp_ref[gi])
        o_ref[:, gi * gc:(gi + 1) * gc] = (mixed * ps_ref[:, gi * gc:(gi + 1) * gc]).astype(bf16)


def _pool(u, w_pool, pool_scale, seq):
    t, ch = u.shape
    g, gc, _ = w_pool.shape
    return pl.pallas_call(
        functools.partial(_pool_kernel, seq=seq),
        out_shape=jax.ShapeDtypeStruct((t, ch), bf16),
        grid=(t // seq,),
        in_specs=[pl.BlockSpec((seq, ch), lambda b: (b, 0)),
                  pl.BlockSpec((g, gc, gc), lambda b: (0, 0, 0)),
                  pl.BlockSpec((1, ch), lambda b: (0, 0))],
        out_specs=pl.BlockSpec((seq, ch), lambda b: (b, 0)),
        compiler_params=_cparams(("arbitrary",), 40),
        name="pool_mixer",
    )(u, w_pool, pool_scale)


def _softmax_parts(scores):
    m = scores[0].max(axis=-1, keepdims=True)
    for s in scores[1:]:
        m = jnp.maximum(m, s.max(axis=-1, keepdims=True))
    ps = [jnp.exp(s - m) for s in scores]
    l = ps[0].sum(axis=-1, keepdims=True)
    for p in ps[1:]:
        l = l + p.sum(axis=-1, keepdims=True)
    return ps, l, m


def _mla_kernel(*refs, has_ctx):
    if has_ctx:
        q_ref, kv_ref, kpe_ref, kvc_ref, kpec_ref, o_ref = refs
    else:
        q_ref, kv_ref, kpe_ref, o_ref = refs
    kpe = kpe_ref[...]
    for hd in range(MLA_HEADS):
        a = 2 * hd * LANES
        q = q_ref[:, a:a + 2 * LANES]
        k = jnp.concatenate([kv_ref[:, a:a + LANES], kpe], axis=1)
        scores = [_dot_nt(q, k)]
        vals = [kv_ref[:, a + LANES:a + 2 * LANES]]
        if has_ctx:
            kc = jnp.concatenate([kvc_ref[:, a:a + LANES], kpec_ref[...]], axis=1)
            scores.append(_dot_nt(q, kc))
            vals.append(kvc_ref[:, a + LANES:a + 2 * LANES])
        ps, l, _ = _softmax_parts(scores)
        o = _dot(ps[0].astype(bf16), vals[0])
        for p, v in zip(ps[1:], vals[1:]):
            o = o + _dot(p.astype(bf16), v)
        o_ref[:, hd * LANES:(hd + 1) * LANES] = (o / l).astype(bf16)


def _mla_attention(q, kv, kpe, ctx, seq):
    t, nq = q.shape
    tq = ROW_TILE
    tps = seq // tq
    in_specs = [pl.BlockSpec((tq, nq), lambda b, j: (b * tps + j, 0)),
                pl.BlockSpec((seq, nq), lambda b, j: (b, 0)),
                pl.BlockSpec((seq, LANES), lambda b, j: (b, 0))]
    args = [q, kv, kpe]
    if ctx is not None:
        kvc, kpec, past = ctx
        in_specs += [pl.BlockSpec((past, nq), lambda b, j: (b, 0)),
                     pl.BlockSpec((past, LANES), lambda b, j: (b, 0))]
        args += [kvc, kpec]
    return pl.pallas_call(
        functools.partial(_mla_kernel, has_ctx=ctx is not None),
        out_shape=jax.ShapeDtypeStruct((t, MLA_HEADS * V_HEAD), bf16),
        grid=(t // seq, tps),
        in_specs=in_specs,
        out_specs=pl.BlockSpec((tq, MLA_HEADS * V_HEAD), lambda b, j: (b * tps + j, 0)),
        compiler_params=_cparams(("arbitrary", "arbitrary"), 48),
        name="mla_attention",
    )(*args)


def _out_ln_kernel(*refs, n_pieces, alpha):
    pieces = refs[:n_pieces]
    w_ref, x_ref, mod_ref, g_ref, b_ref, wrh_ref, wrl_ref, x1_ref, h2_ref, lg_ref = refs[n_pieces:]
    off = 0
    mix = None
    for p in pieces:
        kp = p.shape[1]
        part = _dot(p[...], w_ref[off:off + kp, :])
        mix = part if mix is None else mix + part
        off += kp
    g1 = mod_ref[0, 2:3, :]
    x1 = _layer_norm(alpha * x_ref[...] + g1 * mix, g_ref[...], b_ref[...])
    x1_ref[...] = x1
    h2 = x1 * (1.0 + mod_ref[0, 4:5, :]) + mod_ref[0, 3:4, :]
    h2h = h2.astype(bf16)
    h2_ref[...] = h2h
    h2l = (h2 - h2h.astype(f32)).astype(bf16)
    wrh = wrh_ref[...]
    lg_ref[...] = _dot_nt(wrh, h2h) + _dot_nt(wrh, h2l) + _dot_nt(wrl_ref[...], h2h)


def _out_ln(pieces, w_out, x, mods, ln_g, ln_b, wr_hi, wr_lo, *, alpha, tiles_per_mod):
    t, d = x.shape
    tm = ROW_TILE
    e = wr_hi.shape[0]
    row = lambda i: (i, 0)
    const = lambda i: (0, 0)
    in_specs = [pl.BlockSpec((tm, p.shape[1]), row) for p in pieces]
    in_specs += [pl.BlockSpec(w_out.shape, const), pl.BlockSpec((tm, d), row),
                 pl.BlockSpec((1, 6, d), lambda i: (i // tiles_per_mod, 0, 0)),
                 pl.BlockSpec((1, d), const), pl.BlockSpec((1, d), const),
                 pl.BlockSpec((e, d), const), pl.BlockSpec((e, d), const)]
    return pl.pallas_call(
        functools.partial(_out_ln_kernel, n_pieces=len(pieces), alpha=alpha),
        out_shape=(jax.ShapeDtypeStruct((t, d), f32), jax.ShapeDtypeStruct((t, d), bf16),
                   jax.ShapeDtypeStruct((e, t), f32)),
        grid=(t // tm,),
        in_specs=in_specs,
        out_specs=(pl.BlockSpec((tm, d), row), pl.BlockSpec((tm, d), row), pl.BlockSpec((e, tm), lambda i: (0, i))),
        compiler_params=_cparams(("arbitrary",), 48),
        name="out_proj_ln",
    )(*pieces, w_out, x, mods, ln_g, ln_b, wr_hi, wr_lo)


def _route_kernel(lg_ref, h2_ref, tri_ref, xs_ref, gate_ref, pt_ref, aff_sc, pos_sc, *, cap, seq, n_exp):
    lg = lg_ref[...]
    z = jnp.exp(lg - lg.max(axis=0, keepdims=True))
    aff = z / z.sum(axis=0, keepdims=True)
    bits = pltpu.bitcast(aff, jnp.int32)
    thr = jnp.zeros((n_exp, 1), jnp.int32)
    for b in range(30, -1, -1):
        cand = thr | (1 << b)
        cnt = jnp.sum(jnp.where(bits >= cand, 1.0, 0.0), axis=1, keepdims=True)
        thr = jnp.where(cnt >= cap, cand, thr)
    gt = bits > thr
    eq = bits == thr
    need = cap - jnp.sum(jnp.where(gt, 1.0, 0.0), axis=1, keepdims=True)
    tri = tri_ref[...]
    eq_rank = _dot(jnp.where(eq, 1.0, 0.0).astype(bf16), tri)
    sel = gt | (eq & (eq_rank <= need))
    sel_rank = _dot(jnp.where(sel, 1.0, 0.0).astype(bf16), tri)
    pos = jnp.where(sel, sel_rank - 1.0, -1.0)
    aff_sc[...] = aff
    pos_sc[...] = jnp.full(pos_sc.shape, -1.0, f32)
    pos_sc[0:n_exp, :] = pos

    slot_col = lax.broadcasted_iota(jnp.int32, (cap, seq), 0).astype(f32)

    def dispatch(e, carry):
        hit = pos_sc[pl.ds(e, 1), :] == slot_col
        xs_ref[e] = _dot(jnp.where(hit, 1.0, 0.0).astype(bf16), h2_ref[...]).astype(bf16)
        gate = jnp.sum(jnp.where(hit, aff_sc[pl.ds(e, 1), :], 0.0), axis=1, keepdims=True)
        gate_ref[e] = jnp.broadcast_to(gate, (cap, LANES))
        return carry

    lax.fori_loop(0, n_exp, dispatch, 0)

    pos_t = pos_sc[...].T
    lane = lax.broadcasted_iota(jnp.int32, (1, LANES), 1)
    for j in range(n_exp * cap // LANES):
        if cap >= LANES:
            e = j * LANES // cap
            target = (lane + (j * LANES - e * cap)).astype(f32)
            hit = pos_t[:, e:e + 1] == target
        else:
            per = LANES // cap
            hit = None
            for k in range(per):
                e = j * per + k
                target = jnp.where((lane >= k * cap) & (lane < (k + 1) * cap), lane - k * cap, -7).astype(f32)
                m = pos_t[:, e:e + 1] == target
                hit = m if hit is None else hit | m
        pt_ref[:, j * LANES:(j + 1) * LANES] = jnp.where(hit, 1.0, 0.0).astype(bf16)


def _route(lg_t, h2, tri, *, seq, cap):
    n_exp, t = lg_t.shape
    d = h2.shape[1]
    nb = t // seq
    return pl.pallas_call(
        functools.partial(_route_kernel, cap=cap, seq=seq, n_exp=n_exp),
        out_shape=(jax.ShapeDtypeStruct((n_exp, nb * cap, d), bf16),
                   jax.ShapeDtypeStruct((n_exp, nb * cap, LANES), f32),
                   jax.ShapeDtypeStruct((t, n_exp * cap), bf16)),
        grid=(nb,),
        in_specs=[pl.BlockSpec((n_exp, seq), lambda b: (0, b)),
                  pl.BlockSpec((seq, d), lambda b: (b, 0)),
                  pl.BlockSpec((seq, seq), lambda b: (0, 0))],
        out_specs=(pl.BlockSpec((n_exp, cap, d), lambda b: (0, b, 0)),
                   pl.BlockSpec((n_exp, cap, LANES), lambda b: (0, b, 0)),
                   pl.BlockSpec((seq, n_exp * cap), lambda b: (b, 0))),
        scratch_shapes=[pltpu.VMEM((n_exp, seq), f32), pltpu.VMEM((LANES, seq), f32)],
        compiler_params=_cparams(("arbitrary",), 56),
        name="route_dispatch",
    )(lg_t, h2, tri)


EXPERT_TILE = 512
EXPERT_ROWS = 256


def _expert_kernel(xp_ref, xs_ref, gp_ref, gs_ref, wg_ref, wu_ref, wd_ref, yp_ref, ys_ref,
                   hid_sc, wa_sc, wb_sc, wc_sc, *, nf, tf):
    j = pl.program_id(1)
    mp, ms = xp_ref.shape[1], xs_ref.shape[1]
    cp, cs = min(EXPERT_ROWS, mp), min(EXPERT_ROWS, ms)
    chunks = [(xp_ref, gp_ref, yp_ref, r, r, cp) for r in range(0, mp, cp)]
    chunks += [(xs_ref, gs_ref, ys_ref, r, mp + r, cs) for r in range(0, ms, cs)]

    @pl.when(j < nf)
    def _():
        wa_sc[...] = wg_ref[0, 0].astype(bf16)
        wb_sc[...] = wu_ref[0, 0].astype(bf16)
        for x_ref, _, _, r, m, n in chunks:
            xc = x_ref[0, r:r + n, :]
            g = _dot(xc, wa_sc[...])
            u = _dot(xc, wb_sc[...])
            hid_sc[j, m:m + n, :] = (_silu(g) * u).astype(bf16)

    @pl.when(j >= nf)
    def _():
        wc_sc[...] = wd_ref[0, 0].astype(bf16)
        for _, g_ref, y_ref, r, m, n in chunks:
            acc = _dot(hid_sc[0, m:m + n, :], wc_sc[0:tf, :])
            for f in range(1, nf):
                acc = acc + _dot(hid_sc[f, m:m + n, :], wc_sc[f * tf:(f + 1) * tf, :])
            y_ref[0, r:r + n, :] = (acc * g_ref[0, r:r + n, 0:1]).astype(bf16)


def _experts(xp, xs, gp, gs, w_gate, w_up, w_down, layer):
    n_exp, mp, d = xp.shape
    ms = xs.shape[1]
    ff = w_gate.shape[3]
    tf = tn = EXPERT_TILE
    nf, nn = ff // tf, d // tn
    x_map = lambda e, j: (e, 0, 0)
    up_map = lambda e, j: (layer, e, 0, jnp.minimum(j, nf - 1))
    dn_map = lambda e, j: (layer, e, 0, jnp.maximum(j - nf, 0))
    y_map = lambda e, j: (e, 0, jnp.maximum(j - nf, 0))
    return pl.pallas_call(
        functools.partial(_expert_kernel, nf=nf, tf=tf),
        out_shape=(jax.ShapeDtypeStruct((n_exp, mp, d), bf16), jax.ShapeDtypeStruct((n_exp, ms, d), bf16)),
        grid=(n_exp, nf + nn),
        in_specs=[pl.BlockSpec((1, mp, d), x_map), pl.BlockSpec((1, ms, d), x_map),
                  pl.BlockSpec((1, mp, LANES), x_map), pl.BlockSpec((1, ms, LANES), x_map),
                  pl.BlockSpec((1, 1, d, tf), up_map), pl.BlockSpec((1, 1, d, tf), up_map),
                  pl.BlockSpec((1, 1, ff, tn), dn_map)],
        out_specs=(pl.BlockSpec((1, mp, tn), y_map), pl.BlockSpec((1, ms, tn), y_map)),
        scratch_shapes=[pltpu.VMEM((nf, mp + ms, tf), bf16), pltpu.VMEM((d, tf), bf16),
                        pltpu.VMEM((d, tf), bf16), pltpu.VMEM((ff, tn), bf16)],
        compiler_params=_cparams(("arbitrary", "arbitrary"), 58),
        name="expert_swiglu",
    )(xp, xs, gp, gs, w_gate, w_up, w_down)


def _combine_kernel(pt_ref, y_ref, x_ref, mod_ref, g_ref, b_ref, o_ref, *, alpha):
    n_exp, cap, d = y_ref.shape
    ffn = _dot(pt_ref[...], y_ref[...].reshape(n_exp * cap, d))
    o_ref[...] = _layer_norm(alpha * x_ref[...] + mod_ref[0, 5:6, :] * ffn, g_ref[...], b_ref[...])


def _combine(pt, y, x1, mods, ln_g, ln_b, *, alpha, seq, cap, tiles_per_mod):
    t, d = x1.shape
    n_exp = y.shape[0]
    tm = ROW_TILE
    tps = seq // tm
    row = lambda i: (i, 0)
    const = lambda i: (0, 0)
    return pl.pallas_call(
        functools.partial(_combine_kernel, alpha=alpha),
        out_shape=jax.ShapeDtypeStruct((t, d), f32),
        grid=(t // tm,),
        in_specs=[pl.BlockSpec((tm, n_exp * cap), row),
                  pl.BlockSpec((n_exp, cap, d), lambda i: (0, i // tps, 0)),
                  pl.BlockSpec((tm, d), row),
                  pl.BlockSpec((1, 6, d), lambda i: (i // tiles_per_mod, 0, 0)),
                  pl.BlockSpec((1, d), const), pl.BlockSpec((1, d), const)],
        out_specs=pl.BlockSpec((tm, d), row),
        compiler_params=_cparams(("arbitrary",), 48),
        name="combine_ln",
    )(pt, y, x1, mods, ln_g, ln_b)


def _pad_pairs(chunk, lane):
    swapped = pltpu.roll(chunk, LANES // 2, axis=1)
    lo = lane < LANES // 2
    return [jnp.where(lo, chunk, 0.0), jnp.where(lo, 0.0, swapped),
            jnp.where(lo, swapped, 0.0), jnp.where(lo, 0.0, chunk)]


def _in_odd_kernel(x_ref, mod_ref, win_ref, rc_ref, rsa_ref, rsb_ref,
                   q_ref, ka_ref, va_ref, kf_ref, vf_ref, *, rope, nq, nkv, qscale):
    sh = mod_ref[0, 0:1, :]
    sc = mod_ref[0, 1:2, :]
    h = (x_ref[...] * (1.0 + sc) + sh).astype(bf16)
    proj = _dot(h, win_ref[...])
    k = proj[:, nq:nq + nkv]
    v = proj[:, nq + nkv:nq + 2 * nkv]
    kf_ref[...] = k
    vf_ref[...] = v
    if rope:
        c, sa, sb = rc_ref[...], rsa_ref[...], rsb_ref[...]
    for a in range(0, nq, LANES):
        r = proj[:, a:a + LANES]
        if rope:
            r = _rope(r, c, sa, sb)
        q_ref[:, a:a + LANES] = (r * qscale).astype(bf16)
    lane = lax.broadcasted_iota(jnp.int32, (1, LANES), 1)
    for a in range(0, nkv, LANES):
        kc = k[:, a:a + LANES]
        if rope:
            kc = _rope(kc, c, sa, sb)
        for i, (kp, vp) in enumerate(zip(_pad_pairs(kc, lane), _pad_pairs(v[:, a:a + LANES], lane))):
            o = 4 * a + i * LANES
            ka_ref[:, o:o + LANES] = kp.astype(bf16)
            va_ref[:, o:o + LANES] = vp.astype(bf16)


def _in_odd(x, mods, w_in, rtab, *, rope, tiles_per_mod, tiles_per_seq):
    t, d = x.shape
    n_in = w_in.shape[1]
    nkv = SWA_KV * SWA_HD
    nq = n_in - 2 * nkv
    tm = ROW_TILE
    row = lambda i: (i, 0)
    rmap = lambda i: (i % tiles_per_seq, 0)
    kern = functools.partial(_in_odd_kernel, rope=rope, nq=nq, nkv=nkv, qscale=SWA_HD ** -0.5)
    return pl.pallas_call(
        kern,
        out_shape=(jax.ShapeDtypeStruct((t, nq), bf16), jax.ShapeDtypeStruct((t, 4 * nkv), bf16),
                   jax.ShapeDtypeStruct((t, 4 * nkv), bf16), jax.ShapeDtypeStruct((t, nkv), f32),
                   jax.ShapeDtypeStruct((t, nkv), f32)),
        grid=(t // tm,),
        in_specs=[pl.BlockSpec((tm, d), row),
                  pl.BlockSpec((1, 6, d), lambda i: (i // tiles_per_mod, 0, 0)),
                  pl.BlockSpec((d, n_in), lambda i: (0, 0)),
                  pl.BlockSpec((tm, LANES), rmap), pl.BlockSpec((tm, LANES), rmap), pl.BlockSpec((tm, LANES), rmap)],
        out_specs=(pl.BlockSpec((tm, nq), row), pl.BlockSpec((tm, 4 * nkv), row), pl.BlockSpec((tm, 4 * nkv), row),
                   pl.BlockSpec((tm, nkv), row), pl.BlockSpec((tm, nkv), row)),
        compiler_params=_cparams(("arbitrary",), 48),
        name="in_proj_odd",
    )(x, mods, w_in, *rtab)


def _gqa_kernel(*refs, windowed, seq, tq):
    if windowed:
        q_ref, ka_ref, va_ref, kc_ref, vc_ref, sink_ref, o_ref = refs
    else:
        q_ref, ka_ref, va_ref, sink_ref, o_ref = refs
    n_heads = q_ref.shape[1] // SWA_HD
    group = n_heads // SWA_KV
    if windowed:
        span = tq + 2 * WINDOW
        q0 = pl.program_id(1) * tq
        start = pl.multiple_of(jnp.clip(q0 - WINDOW, 0, seq - span), LANES)
        kpos = start + lax.broadcasted_iota(jnp.int32, (tq, span), 1)
        qpos = q0 + lax.broadcasted_iota(jnp.int32, (tq, span), 0)
        band = jnp.abs(kpos - qpos) <= WINDOW
        rows = pl.ds(start, span)
    else:
        rows = slice(None)
    for hk in range(SWA_KV):
        kpad = [ka_ref[rows, (2 * hk + s) * LANES:(2 * hk + s + 1) * LANES] for s in range(2)]
        vpad = [va_ref[rows, (2 * hk + s) * LANES:(2 * hk + s + 1) * LANES] for s in range(2)]
        if windowed:
            kcp = [kc_ref[:, (2 * hk + s) * LANES:(2 * hk + s + 1) * LANES] for s in range(2)]
            vcp = [vc_ref[:, (2 * hk + s) * LANES:(2 * hk + s + 1) * LANES] for s in range(2)]
        for pr in range(group // 2):
            col = (hk * group // 2 + pr) * LANES
            q = q_ref[:, col:col + LANES]
            out = None
            for s in range(2):
                head = 2 * (hk * group // 2 + pr) + s
                sink = sink_ref[head:head + 1, 0:1]
                sc = _dot_nt(q, kpad[s])
                if windowed:
                    sc = jnp.where(band, sc, -1e30)
                    scores = [sc, _dot_nt(q, kcp[s])]
                    vals = [vpad[s], vcp[s]]
                else:
                    scores, vals = [sc], [vpad[s]]
                m = sink
                for x in scores:
                    m = jnp.maximum(m, x.max(axis=-1, keepdims=True))
                l = jnp.exp(sink - m)
                o = None
                for x, v in zip(scores, vals):
                    p = jnp.exp(x - m)
                    l = l + p.sum(axis=-1, keepdims=True)
                    pv = _dot(p.astype(bf16), v)
                    o = pv if o is None else o + pv
                o = o / l
                out = o if out is None else out + o
            o_ref[:, col:col + LANES] = out.astype(bf16)


def _gqa_attention(q, ka, va, sink, ctx, *, seq, windowed):
    t, nq = q.shape
    nk = ka.shape[1]
    tq = WINDOW if windowed else seq
    tps = seq // tq
    in_specs = [pl.BlockSpec((tq, nq), lambda b, j: (b * tps + j, 0)),
                pl.BlockSpec((seq, nk), lambda b, j: (b, 0)),
                pl.BlockSpec((seq, nk), lambda b, j: (b, 0))]
    args = [q, ka, va]
    if windowed:
        kc, vc, past = ctx
        in_specs += [pl.BlockSpec((past, nk), lambda b, j: (b, 0)), pl.BlockSpec((past, nk), lambda b, j: (b, 0))]
        args += [kc, vc]
    in_specs.append(pl.BlockSpec(sink.shape, lambda b, j: (0, 0)))
    args.append(sink)
    return pl.pallas_call(
        functools.partial(_gqa_kernel, windowed=windowed, seq=seq, tq=tq),
        out_shape=jax.ShapeDtypeStruct((t, nq), bf16),
        grid=(t // seq, tps),
        in_specs=in_specs,
        out_specs=pl.BlockSpec((tq, nq), lambda b, j: (b * tps + j, 0)),
        compiler_params=_cparams(("arbitrary", "arbitrary"), 40),
        name="gqa_attention",
    )(*args)


def _rope_tables(n_tokens, rot_dim):
    rows = n_tokens // GRID_W
    row = jnp.repeat(jnp.arange(rows, dtype=f32), GRID_W)
    col = jnp.tile(jnp.arange(GRID_W, dtype=f32), rows)
    n_freq = rot_dim // 4
    inv = ROPE_BASE ** (-jnp.arange(n_freq, dtype=f32) / n_freq)
    ang = jnp.concatenate([row[:, None] * inv, col[:, None] * inv], -1)
    cos = jnp.repeat(jnp.cos(ang), 2, axis=1)
    sin = jnp.repeat(jnp.sin(ang), 2, axis=1)
    even = (jnp.arange(rot_dim) % 2 == 0)[None, :]
    sa = jnp.where(even, -sin, 0.0)
    sb = jnp.where(even, 0.0, sin)
    reps = LANES // rot_dim
    if reps >= 1:
        return tuple(jnp.tile(a, (1, reps)) for a in (cos, sa, sb))
    raise NotImplementedError


def _pad_kv_pairs(a):
    rows = a.shape[0]
    a = a.reshape(rows, SWA_KV, SWA_HD)
    z = jnp.zeros_like(a)
    return jnp.concatenate([a, z, z, a], axis=-1).reshape(rows, SWA_KV * 4 * SWA_HD)


def _split_hi_lo(w):
    hi = w.astype(bf16)
    return hi, (w - hi.astype(f32)).astype(bf16)


def kernel(x_prompt, x_sample, cache_mla_ckv, cache_mla_kpe, cache_swa_k, cache_swa_v, c, c_ctx, w_ada, b_ada, w_in_even, q_norm, kv_norm, w_uq, w_ukv, w_pool, pool_scale, w_out_even, w_in_odd, sink, w_out_odd, ln1_g, ln1_b, w_router, w_gate, w_up, w_down, ln2_g, ln2_b):
    depth, d = w_ada.shape[0], w_ada.shape[1]
    bp, sp, _ = x_prompt.shape
    bs, ss, _ = x_sample.shape
    past = cache_mla_ckv.shape[2]
    n_exp = w_router.shape[2]
    alpha = (2 * depth) ** 0.25
    q_lora, kv_lora = q_norm.shape[1], kv_norm.shape[1]
    pool_ch = pool_scale.shape[1]
    assert sp % ROW_TILE == 0 and ss % ROW_TILE == 0 and ss % GRID_W == 0
    assert QK_ROPE <= LANES and LANES % SWA_HD == 0 and 2 * SWA_HD == LANES

    n_rows = -(-(1 + bs) // 16) * 16
    cond = jnp.zeros((n_rows, d), f32).at[0].set(c_ctx).at[1:1 + bs].set(c)
    mods = _modulation(cond, w_ada, b_ada)
    mods = mods.reshape(depth, n_rows, 6, d)

    streams = {
        "p": dict(x=x_prompt.reshape(bp * sp, d), b=bp, s=sp, rope=False, mrow=slice(0, 1),
                  tiles_per_mod=bp * sp // ROW_TILE),
        "s": dict(x=x_sample.reshape(bs * ss, d), b=bs, s=ss, rope=True, mrow=slice(1, 1 + bs),
                  tiles_per_mod=ss // ROW_TILE),
    }
    for st in streams.values():
        st["cap"] = max(1, CAP_FACTOR * st["s"] // n_exp)
        st["tri"] = jnp.triu(jnp.ones((st["s"], st["s"]), bf16))
        assert (n_exp * st["cap"]) % LANES == 0 and (LANES % st["cap"] == 0 or st["cap"] % LANES == 0)
        m_rows = st["b"] * st["cap"]
        assert m_rows % min(EXPERT_ROWS, m_rows) == 0 and m_rows % 16 == 0

    rope_mla = _rope_tables(ss, QK_ROPE)
    rope_swa = _rope_tables(ss, SWA_HD)
    ident = (jnp.ones((ROW_TILE, LANES), f32), jnp.zeros((ROW_TILE, LANES), f32), jnp.zeros((ROW_TILE, LANES), f32))

    outs = {"ckv": [], "kpe": [], "k": [], "v": []}
    for l in range(depth):
        wr_hi, wr_lo = _split_hi_lo(w_router[l].T)
        if l % 2 == 0:
            e = l // 2
            wi = w_in_even[e]
            o_pe = q_lora + kv_lora
            w_in = jnp.concatenate([wi[:, :o_pe + QK_ROPE], jnp.zeros((d, LANES - QK_ROPE), f32),
                                    wi[:, o_pe + QK_ROPE:]], axis=1).astype(bf16)
            wq = w_uq[e].reshape(q_lora, MLA_HEADS, QK_NOPE + QK_ROPE)
            wq = jnp.pad(wq, ((0, 0), (0, 0), (0, 2 * LANES - QK_NOPE - QK_ROPE)))
            wq = wq.reshape(q_lora, MLA_HEADS * 2 * LANES).astype(bf16)
            wkv = w_ukv[e].astype(bf16)
            w_out = w_out_even[e].astype(bf16)
            wp = w_pool[e].astype(bf16)
            kvc = _matmul(cache_mla_ckv[:, e].reshape(bs * past, kv_lora), wkv, bf16)
            kpec = jnp.pad(cache_mla_kpe[:, e].reshape(bs * past, QK_ROPE),
                           ((0, 0), (0, LANES - QK_ROPE))).astype(bf16)
        else:
            o = l // 2
            w_in = w_in_odd[o].astype(bf16)
            w_out = w_out_odd[o].astype(bf16)
            sink_b = jnp.broadcast_to(sink[o][:, None], (sink.shape[1], LANES))
            kc = _pad_kv_pairs(cache_swa_k[:, o].reshape(bs * past, SWA_KV * SWA_HD)).astype(bf16)
            vc = _pad_kv_pairs(cache_swa_v[:, o].reshape(bs * past, SWA_KV * SWA_HD)).astype(bf16)

        for name, st in streams.items():
            m = mods[l, st["mrow"]]
            tps = st["s"] // ROW_TILE
            common = dict(rope=st["rope"], tiles_per_mod=st["tiles_per_mod"], tiles_per_seq=tps)
            if l % 2 == 0:
                q, ckv, kv, kpe_raw, kpe_att, u = _in_even(
                    st["x"], m, w_in, q_norm[e][None], kv_norm[e][None], wq, wkv,
                    rope_mla if st["rope"] else ident, **common)
                pooled = _pool(u, wp, pool_scale[e][None], st["s"])
                ctx = (kvc, kpec, past) if st["rope"] else None
                attn = _mla_attention(q, kv, kpe_att, ctx, st["s"])
                pieces = [attn, pooled]
                if not st["rope"]:
                    outs["ckv"].append(ckv.reshape(st["b"], st["s"], kv_lora))
                    outs["kpe"].append(kpe_raw[:, :QK_ROPE].reshape(st["b"], st["s"], QK_ROPE))
            else:
                q, ka, va, kf, vf = _in_odd(st["x"], m, w_in, rope_swa if st["rope"] else ident, **common)
                ctx = (kc, vc, past) if st["rope"] else None
                attn = _gqa_attention(q, ka, va, sink_b, ctx, seq=st["s"], windowed=st["rope"])
                pieces = [attn]
                if not st["rope"]:
                    outs["k"].append(kf.reshape(st["b"], st["s"], SWA_KV, SWA_HD))
                    outs["v"].append(vf.reshape(st["b"], st["s"], SWA_KV, SWA_HD))
            st["x1"], h2, lg_t = _out_ln(pieces, w_out, st["x"], m, ln1_g[l][None], ln1_b[l][None],
                                         wr_hi, wr_lo, alpha=alpha, tiles_per_mod=st["tiles_per_mod"])
            st["xs"], st["gate"], st["pt"] = _route(lg_t, h2, st["tri"], seq=st["s"], cap=st["cap"])

        yp, ys = _experts(streams["p"]["xs"], streams["s"]["xs"], streams["p"]["gate"], streams["s"]["gate"],
                          w_gate, w_up, w_down, l)
        for st, y in ((streams["p"], yp), (streams["s"], ys)):
            st["x"] = _combine(st["pt"], y, st["x1"], mods[l, st["mrow"]], ln2_g[l][None], ln2_b[l][None],
                               alpha=alpha, seq=st["s"], cap=st["cap"], tiles_per_mod=st["tiles_per_mod"])

    y_prompt = streams["p"]["x"].reshape(bp, sp, d)
    y_sample = streams["s"]["x"].reshape(bs, ss, d)
    return (y_prompt, y_sample, jnp.stack(outs["ckv"], axis=1), jnp.stack(outs["kpe"], axis=1),
            jnp.stack(outs["k"], axis=1), jnp.stack(outs["v"], axis=1))
```

```python
import functools
import math

import numpy as np
import jax
import jax.numpy as jnp
from jax import lax
from jax.experimental import pallas as pl
from jax.experimental.pallas import tpu as pltpu

f32 = jnp.float32
bf16 = jnp.bfloat16

MLA_HEADS = 8
QK_NOPE = 128
QK_ROPE = 64
V_HEAD = 128
POOL_WINDOWS = (2, 4, 8, 16)
SWA_KV = 4
SWA_HD = 64
WINDOW = 128
CAP_FACTOR = 2
GRID_W = 64
ROPE_BASE = 10000.0
RMS_EPS = 1e-6
LN_EPS = 1e-5

LANES = 128
ROW_TILE = 256
MIB = 1024 * 1024


def _cparams(sem, vmem_mib):
    return pltpu.CompilerParams(dimension_semantics=sem, vmem_limit_bytes=vmem_mib * MIB)


def _dot(a, b):
    return jnp.dot(a, b, preferred_element_type=f32)


def _dot_nt(a, b):
    return lax.dot_general(a, b, (((1,), (1,)), ((), ())), preferred_element_type=f32)


def _silu(x):
    return x / (1.0 + jnp.exp(-x))


def _rope(r, c, sa, sb):
    return r * c + pltpu.roll(r, LANES - 1, axis=1) * sa + pltpu.roll(r, 1, axis=1) * sb


def _layer_norm(z, g, b):
    mu = jnp.mean(z, axis=-1, keepdims=True)
    zc = z - mu
    var = jnp.mean(zc * zc, axis=-1, keepdims=True)
    return zc * lax.rsqrt(var + LN_EPS) * g + b


def _mod_kernel(c_ref, w_ref, b_ref, o_ref):
    a = _silu(c_ref[...]).astype(bf16)
    o_ref[0] = _dot(a, w_ref[0].astype(bf16)) + b_ref[0]


def _modulation(cond, w_ada, b_ada):
    depth, d, n = w_ada.shape
    r = cond.shape[0]
    tn = 1024
    return pl.pallas_call(
        _mod_kernel,
        out_shape=jax.ShapeDtypeStruct((depth, r, n), f32),
        grid=(depth, n // tn),
        in_specs=[pl.BlockSpec((r, d), lambda l, j: (0, 0)),
                  pl.BlockSpec((1, d, tn), lambda l, j: (l, 0, j)),
                  pl.BlockSpec((1, 1, tn), lambda l, j: (l, 0, j))],
        out_specs=pl.BlockSpec((1, r, tn), lambda l, j: (l, 0, j)),
        compiler_params=_cparams(("arbitrary", "arbitrary"), 40),
        name="adaln_modulation",
    )(cond, w_ada, b_ada.reshape(depth, 1, n))


def _in_even_kernel(x_ref, mod_ref, win_ref, qn_ref, kvn_ref, wuq_ref, wukv_ref, rc_ref, rsa_ref, rsb_ref,
                    q_ref, ckv_ref, kv_ref, kper_ref, kpea_ref, u_ref, *, rope, q_lora, kv_lora, qscale):
    sh = mod_ref[0, 0:1, :]
    sc = mod_ref[0, 1:2, :]
    h = (x_ref[...] * (1.0 + sc) + sh).astype(bf16)
    proj = _dot(h, win_ref[...])
    o_kv = q_lora
    o_pe = q_lora + kv_lora
    o_u = o_pe + LANES
    pq = proj[:, 0:o_kv]
    pkv = proj[:, o_kv:o_pe]
    kpe = proj[:, o_pe:o_u]
    u_ref[...] = proj[:, o_u:]
    cq = pq * lax.rsqrt(jnp.mean(pq * pq, axis=-1, keepdims=True) + RMS_EPS) * qn_ref[...]
    ckv = pkv * lax.rsqrt(jnp.mean(pkv * pkv, axis=-1, keepdims=True) + RMS_EPS) * kvn_ref[...]
    ckv_ref[...] = ckv
    kv_ref[...] = _dot(ckv.astype(bf16), wukv_ref[...]).astype(bf16)
    kper_ref[...] = kpe
    q = _dot(cq.astype(bf16), wuq_ref[...]) * qscale
    if rope:
        c, sa, sb = rc_ref[...], rsa_ref[...], rsb_ref[...]
        kpea_ref[...] = _rope(kpe, c, sa, sb).astype(bf16)
    else:
        kpea_ref[...] = kpe.astype(bf16)
    for hd in range(MLA_HEADS):
        a = 2 * hd * LANES
        q_ref[:, a:a + LANES] = q[:, a:a + LANES].astype(bf16)
        r = q[:, a + LANES:a + 2 * LANES]
        if rope:
            r = _rope(r, c, sa, sb)
        q_ref[:, a + LANES:a + 2 * LANES] = r.astype(bf16)


def _in_even(x, mods, w_in, qn, kvn, wuq, wukv, rtab, *, rope, tiles_per_mod, tiles_per_seq):
    t, d = x.shape
    q_lora, kv_lora = qn.shape[1], kvn.shape[1]
    n_in = w_in.shape[1]
    n_u = n_in - q_lora - kv_lora - LANES
    nq, nkv = wuq.shape[1], wukv.shape[1]
    tm = ROW_TILE
    row = lambda i: (i, 0)
    const = lambda i: (0, 0)
    rmap = lambda i: (i % tiles_per_seq, 0)
    kern = functools.partial(_in_even_kernel, rope=rope, q_lora=q_lora, kv_lora=kv_lora,
                             qscale=(QK_NOPE + QK_ROPE) ** -0.5)
    return pl.pallas_call(
        kern,
        out_shape=(jax.ShapeDtypeStruct((t, nq), bf16), jax.ShapeDtypeStruct((t, kv_lora), f32),
                   jax.ShapeDtypeStruct((t, nkv), bf16), jax.ShapeDtypeStruct((t, LANES), f32),
                   jax.ShapeDtypeStruct((t, LANES), bf16), jax.ShapeDtypeStruct((t, n_u), f32)),
        grid=(t // tm,),
        in_specs=[pl.BlockSpec((tm, d), row),
                  pl.BlockSpec((1, 6, d), lambda i: (i // tiles_per_mod, 0, 0)),
                  pl.BlockSpec((d, n_in), const),
                  pl.BlockSpec((1, q_lora), const), pl.BlockSpec((1, kv_lora), const),
                  pl.BlockSpec((q_lora, nq), const), pl.BlockSpec((kv_lora, nkv), const),
                  pl.BlockSpec((tm, LANES), rmap), pl.BlockSpec((tm, LANES), rmap), pl.BlockSpec((tm, LANES), rmap)],
        out_specs=(pl.BlockSpec((tm, nq), row), pl.BlockSpec((tm, kv_lora), row), pl.BlockSpec((tm, nkv), row),
                   pl.BlockSpec((tm, LANES), row), pl.BlockSpec((tm, LANES), row), pl.BlockSpec((tm, n_u), row)),
        compiler_params=_cparams(("arbitrary",), 48),
        name="in_proj_even",
    )(x, mods, w_in, qn, kvn, wuq, wukv, *rtab)


def _mm_kernel(x_ref, w_ref, o_ref):
    o_ref[...] = _dot(x_ref[...].astype(bf16), w_ref[...]).astype(o_ref.dtype)


def _matmul(x, w, out_dtype, tm=512):
    m, k = x.shape
    n = w.shape[1]
    tm = min(tm, m)
    assert m % tm == 0
    return pl.pallas_call(
        _mm_kernel,
        out_shape=jax.ShapeDtypeStruct((m, n), out_dtype),
        grid=(m // tm,),
        in_specs=[pl.BlockSpec((tm, k), lambda i: (i, 0)), pl.BlockSpec((k, n), lambda i: (0, 0))],
        out_specs=pl.BlockSpec((tm, n), lambda i: (i, 0)),
        compiler_params=_cparams(("arbitrary",), 24),
        name="ctx_kv_up",
    )(x, w)


def _pool_kernel(u_ref, wp_ref, ps_ref, o_ref, *, seq):
    gc = wp_ref.shape[1]
    t = lax.broadcasted_iota(jnp.int32, (seq, 1), 0)

    def down(a, k):
        return jnp.where(t >= k, pltpu.roll(a, k, axis=0), 0.0)

    def up(a, k):
        return jnp.where(t < seq - k, pltpu.roll(a, seq - k, axis=0), 0.0)

    for gi, w in enumerate(POOL_WINDOWS):
        half = w // 2
        a = u_ref[:, gi * gc:(gi + 1) * gc]
        trail, lead, k = a, a, 1
        while k < half:
            trail = trail + down(trail, k)
            lead = lead + up(lead, k)
            k *= 2
        win = down(trail, 1) + lead
        cnt = (jnp.minimum(t + half, seq) - jnp.maximum(t - half, 0)).astype(f32)
        pooled = win / cnt - a
        mixed = _dot(pooled.astype(bf16), wp_ref[gi])
        o_ref[:, gi * gc:(gi + 1) * gc] = (mixed * ps_ref[:, gi * gc:(gi + 1) * gc]).astype(bf16)


def _pool(u, w_pool, pool_scale, seq):
    t, ch = u.shape
    g, gc, _ = w_pool.shape
    return pl.pallas_call(
        functools.partial(_pool_kernel, seq=seq),
        out_shape=jax.ShapeDtypeStruct((t, ch), bf16),
        grid=(t // seq,),
        in_specs=[pl.BlockSpec((seq, ch), lambda b: (b, 0)),
                  pl.BlockSpec((g, gc, gc), lambda b: (0, 0, 0)),
                  pl.BlockSpec((1, ch), lambda b: (0, 0))],
        out_specs=pl.BlockSpec((seq, ch), lambda b: (b, 0)),
        compiler_params=_cparams(("arbitrary",), 40),
        name="pool_mixer",
    )(u, w_pool, pool_scale)


def _softmax_parts(scores):
    m = scores[0].max(axis=-1, keepdims=True)
    for s in scores[1:]:
        m = jnp.maximum(m, s.max(axis=-1, keepdims=True))
    ps = [jnp.exp(s - m) for s in scores]
    l = ps[0].sum(axis=-1, keepdims=True)
    for p in ps[1:]:
        l = l + p.sum(axis=-1, keepdims=True)
    return ps, l, m


def _mla_kernel(*refs, has_ctx):
    if has_ctx:
        q_ref, kv_ref, kpe_ref, kvc_ref, kpec_ref, o_ref = refs
    else:
        q_ref, kv_ref, kpe_ref, o_ref = refs
    kpe = kpe_ref[...]
    for hd in range(MLA_HEADS):
        a = 2 * hd * LANES
        q = q_ref[:, a:a + 2 * LANES]
        k = jnp.concatenate([kv_ref[:, a:a + LANES], kpe], axis=1)
        scores = [_dot_nt(q, k)]
        vals = [kv_ref[:, a + LANES:a + 2 * LANES]]
        if has_ctx:
            kc = jnp.concatenate([kvc_ref[:, a:a + LANES], kpec_ref[...]], axis=1)
            scores.append(_dot_nt(q, kc))
            vals.append(kvc_ref[:, a + LANES:a + 2 * LANES])
        ps, l, _ = _softmax_parts(scores)
        o = _dot(ps[0].astype(bf16), vals[0])
        for p, v in zip(ps[1:], vals[1:]):
            o = o + _dot(p.astype(bf16), v)
        o_ref[:, hd * LANES:(hd + 1) * LANES] = (o / l).astype(bf16)


def _mla_attention(q, kv, kpe, ctx, seq):
    t, nq = q.shape
    tq = ROW_TILE
    tps = seq // tq
    in_specs = [pl.BlockSpec((tq, nq), lambda b, j: (b * tps + j, 0)),
                pl.BlockSpec((seq, nq), lambda b, j: (b, 0)),
                pl.BlockSpec((seq, LANES), lambda b, j: (b, 0))]
    args = [q, kv, kpe]
    if ctx is not None:
        kvc, kpec, past = ctx
        in_specs += [pl.BlockSpec((past, nq), lambda b, j: (b, 0)),
                     pl.BlockSpec((past, LANES), lambda b, j: (b, 0))]
        args += [kvc, kpec]
    return pl.pallas_call(
        functools.partial(_mla_kernel, has_ctx=ctx is not None),
        out_shape=jax.ShapeDtypeStruct((t, MLA_HEADS * V_HEAD), bf16),
        grid=(t // seq, tps),
        in_specs=in_specs,
        out_specs=pl.BlockSpec((tq, MLA_HEADS * V_HEAD), lambda b, j: (b * tps + j, 0)),
        compiler_params=_cparams(("arbitrary", "arbitrary"), 48),
        name="mla_attention",
    )(*args)


SUB_TILES = 2


def _out_ln_kernel(*refs, n_pieces, alpha):
    pieces = refs[:n_pieces]
    w_ref, x_ref, mod_ref, g_ref, b_ref, wrh_ref, wrl_ref, x1_ref, h2_ref, lg_ref = refs[n_pieces:]
    g1 = mod_ref[0, 2:3, :]
    sc2 = 1.0 + mod_ref[0, 4:5, :]
    sh2 = mod_ref[0, 3:4, :]
    wrh = wrh_ref[...]
    for r in range(0, x_ref.shape[0], ROW_TILE):
        rows = slice(r, r + ROW_TILE)
        off = 0
        mix = None
        for p in pieces:
            kp = p.shape[1]
            part = _dot(p[rows, :], w_ref[off:off + kp, :])
            mix = part if mix is None else mix + part
            off += kp
        x1 = _layer_norm(alpha * x_ref[rows, :] + g1 * mix, g_ref[...], b_ref[...])
        x1_ref[rows, :] = x1
        h2 = x1 * sc2 + sh2
        h2h = h2.astype(bf16)
        h2_ref[rows, :] = h2h
        h2l = (h2 - h2h.astype(f32)).astype(bf16)
        lg_ref[rows, :] = _dot(h2h, wrh) + _dot(h2l, wrh) + _dot(h2h, wrl_ref[...])


def _out_ln(pieces, w_out, x, mods, ln_g, ln_b, wr_hi, wr_lo, *, alpha, rows_per_mod):
    t, d = x.shape
    tm = ROW_TILE * SUB_TILES
    assert t % tm == 0 and rows_per_mod % tm == 0
    row = lambda i: (i, 0)
    const = lambda i: (0, 0)
    in_specs = [pl.BlockSpec((tm, p.shape[1]), row) for p in pieces]
    in_specs += [pl.BlockSpec(w_out.shape, const), pl.BlockSpec((tm, d), row),
                 pl.BlockSpec((1, 6, d), lambda i: (i // (rows_per_mod // tm), 0, 0)),
                 pl.BlockSpec((1, d), const), pl.BlockSpec((1, d), const),
                 pl.BlockSpec((d, LANES), const), pl.BlockSpec((d, LANES), const)]
    return pl.pallas_call(
        functools.partial(_out_ln_kernel, n_pieces=len(pieces), alpha=alpha),
        out_shape=(jax.ShapeDtypeStruct((t, d), f32), jax.ShapeDtypeStruct((t, d), bf16),
                   jax.ShapeDtypeStruct((t, LANES), f32)),
        grid=(t // tm,),
        in_specs=in_specs,
        out_specs=(pl.BlockSpec((tm, d), row), pl.BlockSpec((tm, d), row), pl.BlockSpec((tm, LANES), row)),
        compiler_params=_cparams(("arbitrary",), 56),
        name="out_proj_ln",
    )(*pieces, w_out, x, mods, ln_g, ln_b, wr_hi, wr_lo)


SEARCH_BITS = 4
ROUTE_TOKENS = 1024


def _kth_largest_bits(bits, cap):
    thr = jnp.zeros((bits.shape[0], 1), jnp.int32)
    hi = 31
    while hi > 0:
        lo = max(hi - SEARCH_BITS, 0) if hi % SEARCH_BITS == 0 else hi - hi % SEARCH_BITS
        digit = jnp.zeros_like(thr)
        for c in range(1, 1 << (hi - lo)):
            cnt = jnp.sum(jnp.where(bits >= (thr | (c << lo)), 1.0, 0.0), axis=1, keepdims=True)
            digit = digit + jnp.where(cnt >= cap, 1, 0)
        thr = thr | (digit * (1 << lo))
        hi = lo
    return thr


def _lane_prefix(mask, tri):
    n_rows, n = mask.shape
    blocks = [mask[:, k:k + LANES] for k in range(0, n, LANES)]
    local = _dot(jnp.concatenate(blocks, axis=0).astype(bf16), tri)
    out, carry = [], None
    for i in range(len(blocks)):
        blk = local[i * n_rows:(i + 1) * n_rows, :]
        out.append(blk if carry is None else blk + carry)
        total = blk[:, LANES - 1:LANES]
        carry = total if carry is None else carry + total
    return jnp.concatenate(out, axis=1)


def _route_kernel(lg_ref, h2_ref, tri_ref, xs_ref, gate_ref, pt_ref, aff_sc, pos_sc, *, cap, seq, n_exp, nsb):
    tri = tri_ref[...]
    pos_sc[...] = jnp.full(pos_sc.shape, -1.0, f32)
    for sb in range(nsb):
        lg = lg_ref[sb * seq:(sb + 1) * seq, :].T[0:n_exp, :]
        z = jnp.exp(lg - lg.max(axis=0, keepdims=True))
        aff = z / z.sum(axis=0, keepdims=True)
        bits = pltpu.bitcast(aff, jnp.int32)
        thr = _kth_largest_bits(bits, cap)
        gt = bits > thr
        eq = bits == thr
        need = cap - jnp.sum(jnp.where(gt, 1.0, 0.0), axis=1, keepdims=True)
        eq_rank = _lane_prefix(jnp.where(eq, 1.0, 0.0), tri)
        sel = gt | (eq & (eq_rank <= need))
        sel_rank = _lane_prefix(jnp.where(sel, 1.0, 0.0), tri)
        aff_sc[sb] = aff
        pos_sc[sb, 0:n_exp, :] = jnp.where(sel, sel_rank - 1.0, -1.0)

    slot_col = lax.broadcasted_iota(jnp.int32, (cap, seq), 0).astype(f32)

    def dispatch(e, carry):
        for sb in range(nsb):
            hit = pos_sc[sb, pl.ds(e, 1), :] == slot_col
            rows = slice(sb * cap, (sb + 1) * cap)
            xs_ref[e, rows, :] = _dot(jnp.where(hit, 1.0, 0.0).astype(bf16),
                                      h2_ref[sb * seq:(sb + 1) * seq, :]).astype(bf16)
            gate = jnp.sum(jnp.where(hit, aff_sc[sb, pl.ds(e, 1), :], 0.0), axis=1, keepdims=True)
            gate_ref[e, rows, :] = jnp.broadcast_to(gate, (cap, LANES))
        return carry

    lax.fori_loop(0, n_exp, dispatch, 0)

    lane = lax.broadcasted_iota(jnp.int32, (1, LANES), 1)
    for sb in range(nsb):
        pos_t = pos_sc[sb].T
        for j in range(n_exp * cap // LANES):
            if cap >= LANES:
                e = j * LANES // cap
                target = (lane + (j * LANES - e * cap)).astype(f32)
                hit = pos_t[:, e:e + 1] == target
            else:
                per = LANES // cap
                hit = None
                for k in range(per):
                    e = j * per + k
                    target = jnp.where((lane >= k * cap) & (lane < (k + 1) * cap), lane - k * cap, -7).astype(f32)
                    m = pos_t[:, e:e + 1] == target
                    hit = m if hit is None else hit | m
            pt_ref[sb * seq:(sb + 1) * seq, j * LANES:(j + 1) * LANES] = jnp.where(hit, 1.0, 0.0).astype(bf16)


def _route(lg, h2, tri, *, seq, cap, nsb, n_exp):
    t, d = h2.shape
    nb = t // seq
    assert nb % nsb == 0
    return pl.pallas_call(
        functools.partial(_route_kernel, cap=cap, seq=seq, n_exp=n_exp, nsb=nsb),
        out_shape=(jax.ShapeDtypeStruct((n_exp, nb * cap, d), bf16),
                   jax.ShapeDtypeStruct((n_exp, nb * cap, LANES), f32),
                   jax.ShapeDtypeStruct((t, n_exp * cap), bf16)),
        grid=(nb // nsb,),
        in_specs=[pl.BlockSpec((nsb * seq, LANES), lambda b: (b, 0)),
                  pl.BlockSpec((nsb * seq, d), lambda b: (b, 0)),
                  pl.BlockSpec((LANES, LANES), lambda b: (0, 0))],
        out_specs=(pl.BlockSpec((n_exp, nsb * cap, d), lambda b: (0, b, 0)),
                   pl.BlockSpec((n_exp, nsb * cap, LANES), lambda b: (0, b, 0)),
                   pl.BlockSpec((nsb * seq, n_exp * cap), lambda b: (b, 0))),
        scratch_shapes=[pltpu.VMEM((nsb, n_exp, seq), f32), pltpu.VMEM((nsb, LANES, seq), f32)],
        compiler_params=_cparams(("arbitrary",), 56),
        name="route_dispatch",
    )(lg, h2, tri)


EXPERT_TILE = 512
EXPERT_ROWS = 256


def _expert_kernel(xp_ref, xs_ref, gp_ref, gs_ref, wg_ref, wu_ref, wd_ref, yp_ref, ys_ref,
                   hid_sc, wa_sc, wb_sc, wc_sc, *, nf, tf):
    j = pl.program_id(1)
    mp, ms = xp_ref.shape[1], xs_ref.shape[1]
    cp, cs = min(EXPERT_ROWS, mp), min(EXPERT_ROWS, ms)
    chunks = [(xp_ref, gp_ref, yp_ref, r, r, cp) for r in range(0, mp, cp)]
    chunks += [(xs_ref, gs_ref, ys_ref, r, mp + r, cs) for r in range(0, ms, cs)]

    @pl.when(j < nf)
    def _():
        wa_sc[...] = wg_ref[0, 0].astype(bf16)
        wb_sc[...] = wu_ref[0, 0].astype(bf16)
        for x_ref, _, _, r, m, n in chunks:
            xc = x_ref[0, r:r + n, :]
            g = _dot(xc, wa_sc[...])
            u = _dot(xc, wb_sc[...])
            hid_sc[j, m:m + n, :] = (_silu(g) * u).astype(bf16)

    @pl.when(j >= nf)
    def _():
        wc_sc[...] = wd_ref[0, 0].astype(bf16)
        for _, g_ref, y_ref, r, m, n in chunks:
            acc = _dot(hid_sc[0, m:m + n, :], wc_sc[0:tf, :])
            for f in range(1, nf):
                acc = acc + _dot(hid_sc[f, m:m + n, :], wc_sc[f * tf:(f + 1) * tf, :])
            y_ref[0, r:r + n, :] = (acc * g_ref[0, r:r + n, 0:1]).astype(bf16)


def _experts(xp, xs, gp, gs, w_gate, w_up, w_down, layer):
    n_exp, mp, d = xp.shape
    ms = xs.shape[1]
    ff = w_gate.shape[3]
    tf = tn = EXPERT_TILE
    nf, nn = ff // tf, d // tn
    x_map = lambda e, j: (e, 0, 0)
    up_map = lambda e, j: (layer, e, 0, jnp.minimum(j, nf - 1))
    dn_map = lambda e, j: (layer, e, 0, jnp.maximum(j - nf, 0))
    y_map = lambda e, j: (e, 0, jnp.maximum(j - nf, 0))
    return pl.pallas_call(
        functools.partial(_expert_kernel, nf=nf, tf=tf),
        out_shape=(jax.ShapeDtypeStruct((n_exp, mp, d), bf16), jax.ShapeDtypeStruct((n_exp, ms, d), bf16)),
        grid=(n_exp, nf + nn),
        in_specs=[pl.BlockSpec((1, mp, d), x_map), pl.BlockSpec((1, ms, d), x_map),
                  pl.BlockSpec((1, mp, LANES), x_map), pl.BlockSpec((1, ms, LANES), x_map),
                  pl.BlockSpec((1, 1, d, tf), up_map), pl.BlockSpec((1, 1, d, tf), up_map),
                  pl.BlockSpec((1, 1, ff, tn), dn_map)],
        out_specs=(pl.BlockSpec((1, mp, tn), y_map), pl.BlockSpec((1, ms, tn), y_map)),
        scratch_shapes=[pltpu.VMEM((nf, mp + ms, tf), bf16), pltpu.VMEM((d, tf), bf16),
                        pltpu.VMEM((d, tf), bf16), pltpu.VMEM((ff, tn), bf16)],
        compiler_params=_cparams(("arbitrary", "arbitrary"), 58),
        name="expert_swiglu",
    )(xp, xs, gp, gs, w_gate, w_up, w_down)


def _combine_kernel(pt_ref, y_ref, x_ref, mod_ref, g_ref, b_ref, o_ref, *, alpha, seq, cap):
    n_exp, _, d = y_ref.shape
    g2 = mod_ref[0, 5:6, :]
    for r in range(0, x_ref.shape[0], ROW_TILE):
        rows = slice(r, r + ROW_TILE)
        sq = r // seq
        y = y_ref[:, sq * cap:(sq + 1) * cap, :].reshape(n_exp * cap, d)
        ffn = _dot(pt_ref[rows, :], y)
        o_ref[rows, :] = _layer_norm(alpha * x_ref[rows, :] + g2 * ffn, g_ref[...], b_ref[...])


def _combine(pt, y, x1, mods, ln_g, ln_b, *, alpha, seq, cap, rows_per_mod):
    t, d = x1.shape
    n_exp = y.shape[0]
    tm = ROW_TILE * SUB_TILES
    assert t % tm == 0 and rows_per_mod % tm == 0 and (tm % seq == 0 or seq % tm == 0)
    spt = max(1, tm // seq)
    tps = max(1, seq // tm)
    row = lambda i: (i, 0)
    const = lambda i: (0, 0)
    return pl.pallas_call(
        functools.partial(_combine_kernel, alpha=alpha, seq=seq, cap=cap),
        out_shape=jax.ShapeDtypeStruct((t, d), f32),
        grid=(t // tm,),
        in_specs=[pl.BlockSpec((tm, n_exp * cap), row),
                  pl.BlockSpec((n_exp, spt * cap, d), lambda i: (0, i // tps, 0)),
                  pl.BlockSpec((tm, d), row),
                  pl.BlockSpec((1, 6, d), lambda i: (i // (rows_per_mod // tm), 0, 0)),
                  pl.BlockSpec((1, d), const), pl.BlockSpec((1, d), const)],
        out_specs=pl.BlockSpec((tm, d), row),
        compiler_params=_cparams(("arbitrary",), 56),
        name="combine_ln",
    )(pt, y, x1, mods, ln_g, ln_b)


def _pad_pairs(chunk, lane):
    swapped = pltpu.roll(chunk, LANES // 2, axis=1)
    lo = lane < LANES // 2
    return [jnp.where(lo, chunk, 0.0), jnp.where(lo, 0.0, swapped),
            jnp.where(lo, swapped, 0.0), jnp.where(lo, 0.0, chunk)]


def _in_odd_kernel(x_ref, mod_ref, win_ref, rc_ref, rsa_ref, rsb_ref,
                   q_ref, ka_ref, va_ref, kf_ref, vf_ref, *, rope, nq, nkv, qscale, tqa):
    sh = mod_ref[0, 0:1, :]
    sc = mod_ref[0, 1:2, :]
    h = (x_ref[...] * (1.0 + sc) + sh).astype(bf16)
    proj = _dot(h, win_ref[...])
    k = proj[:, nq:nq + nkv]
    v = proj[:, nq + nkv:nq + 2 * nkv]
    kf_ref[...] = k
    vf_ref[...] = v
    if rope:
        c, sa, sb = rc_ref[...], rsa_ref[...], rsb_ref[...]
    pairs = nq // LANES // SWA_KV
    for ci in range(nq // LANES):
        r = proj[:, ci * LANES:(ci + 1) * LANES]
        if rope:
            r = _rope(r, c, sa, sb)
        r = (r * qscale).astype(bf16)
        hk, pr = ci // pairs, ci % pairs
        for tb in range(q_ref.shape[0]):
            q_ref[tb, hk, pr * tqa:(pr + 1) * tqa, :] = r[tb * tqa:(tb + 1) * tqa, :]
    lane = lax.broadcasted_iota(jnp.int32, (1, LANES), 1)
    for a in range(0, nkv, LANES):
        kc = k[:, a:a + LANES]
        if rope:
            kc = _rope(kc, c, sa, sb)
        for i, (kp, vp) in enumerate(zip(_pad_pairs(kc, lane), _pad_pairs(v[:, a:a + LANES], lane))):
            o = 4 * a + i * LANES
            ka_ref[:, o:o + LANES] = kp.astype(bf16)
            va_ref[:, o:o + LANES] = vp.astype(bf16)


def _in_odd(x, mods, w_in, rtab, *, rope, tqa, tiles_per_mod, tiles_per_seq):
    t, d = x.shape
    n_in = w_in.shape[1]
    nkv = SWA_KV * SWA_HD
    nq = n_in - 2 * nkv
    tm = ROW_TILE
    assert tm % tqa == 0
    q_rows = nq // LANES // SWA_KV * tqa
    row = lambda i: (i, 0)
    rmap = lambda i: (i % tiles_per_seq, 0)
    kern = functools.partial(_in_odd_kernel, rope=rope, nq=nq, nkv=nkv, qscale=SWA_HD ** -0.5, tqa=tqa)
    return pl.pallas_call(
        kern,
        out_shape=(jax.ShapeDtypeStruct((t // tqa, SWA_KV, q_rows, LANES), bf16),
                   jax.ShapeDtypeStruct((t, 4 * nkv), bf16),
                   jax.ShapeDtypeStruct((t, 4 * nkv), bf16), jax.ShapeDtypeStruct((t, nkv), f32),
                   jax.ShapeDtypeStruct((t, nkv), f32)),
        grid=(t // tm,),
        in_specs=[pl.BlockSpec((tm, d), row),
                  pl.BlockSpec((1, 6, d), lambda i: (i // tiles_per_mod, 0, 0)),
                  pl.BlockSpec((d, n_in), lambda i: (0, 0)),
                  pl.BlockSpec((tm, LANES), rmap), pl.BlockSpec((tm, LANES), rmap), pl.BlockSpec((tm, LANES), rmap)],
        out_specs=(pl.BlockSpec((tm // tqa, SWA_KV, q_rows, LANES), lambda i: (i, 0, 0, 0)),
                   pl.BlockSpec((tm, 4 * nkv), row), pl.BlockSpec((tm, 4 * nkv), row),
                   pl.BlockSpec((tm, nkv), row), pl.BlockSpec((tm, nkv), row)),
        compiler_params=_cparams(("arbitrary",), 48),
        name="in_proj_odd",
    )(x, mods, w_in, *rtab)


def _gqa_kernel(*refs, windowed, seq, tq):
    if windowed:
        q_ref, ka_ref, va_ref, kc_ref, vc_ref, sink_ref, o_ref = refs
    else:
        q_ref, ka_ref, va_ref, sink_ref, o_ref = refs
    pairs = q_ref.shape[2] // tq
    if windowed:
        span = tq + 2 * WINDOW
        q0 = pl.program_id(1) * tq
        start = pl.multiple_of(jnp.clip(q0 - WINDOW, 0, seq - span), LANES)
        kpos = start + lax.broadcasted_iota(jnp.int32, (tq, span), 1)
        qpos = q0 + lax.broadcasted_iota(jnp.int32, (tq, span), 0)
        bias1 = jnp.where(jnp.abs(kpos - qpos) <= WINDOW, 0.0, -1e30)
        bias = jnp.concatenate([bias1] * pairs, axis=0)
        rows = pl.ds(start, span)
    else:
        rows = slice(None)
    for hk in range(SWA_KV):
        q = q_ref[0, hk]
        out = None
        for s in range(2):
            cs = slice((2 * hk + s) * LANES, (2 * hk + s + 1) * LANES)
            heads = [2 * (hk * pairs + pr) + s for pr in range(pairs)]
            sink = jnp.concatenate([jnp.broadcast_to(sink_ref[h:h + 1, 0:1], (tq, 1)) for h in heads], axis=0)
            sc = _dot_nt(q, ka_ref[rows, cs])
            if windowed:
                scores = [sc + bias, _dot_nt(q, kc_ref[:, cs])]
                vals = [va_ref[rows, cs], vc_ref[:, cs]]
            else:
                scores, vals = [sc], [va_ref[rows, cs]]
            m = sink
            for x in scores:
                m = jnp.maximum(m, x.max(axis=-1, keepdims=True))
            l = jnp.exp(sink - m)
            o = None
            for x, v in zip(scores, vals):
                p = jnp.exp(x - m)
                l = l + p.sum(axis=-1, keepdims=True)
                pv = _dot(p.astype(bf16), v)
                o = pv if o is None else o + pv
            o = o / l
            out = o if out is None else out + o
        for pr in range(pairs):
            col = (hk * pairs + pr) * LANES
            o_ref[:, col:col + LANES] = out[pr * tq:(pr + 1) * tq, :].astype(bf16)


def _gqa_attention(q, ka, va, sink, ctx, *, seq, windowed):
    t, nk = ka.shape
    tq = WINDOW if windowed else seq
    tps = seq // tq
    q_rows = q.shape[2]
    nq = q_rows // tq * SWA_KV * LANES
    in_specs = [pl.BlockSpec((1, SWA_KV, q_rows, LANES), lambda b, j: (b * tps + j, 0, 0, 0)),
                pl.BlockSpec((seq, nk), lambda b, j: (b, 0)),
                pl.BlockSpec((seq, nk), lambda b, j: (b, 0))]
    args = [q, ka, va]
    if windowed:
        kc, vc, past = ctx
        in_specs += [pl.BlockSpec((past, nk), lambda b, j: (b, 0)), pl.BlockSpec((past, nk), lambda b, j: (b, 0))]
        args += [kc, vc]
    in_specs.append(pl.BlockSpec(sink.shape, lambda b, j: (0, 0)))
    args.append(sink)
    return pl.pallas_call(
        functools.partial(_gqa_kernel, windowed=windowed, seq=seq, tq=tq),
        out_shape=jax.ShapeDtypeStruct((t, nq), bf16),
        grid=(t // seq, tps),
        in_specs=in_specs,
        out_specs=pl.BlockSpec((tq, nq), lambda b, j: (b * tps + j, 0)),
        compiler_params=_cparams(("arbitrary", "arbitrary"), 40),
        name="gqa_attention",
    )(*args)


def _rope_tables(n_tokens, rot_dim):
    rows = n_tokens // GRID_W
    row = jnp.repeat(jnp.arange(rows, dtype=f32), GRID_W)
    col = jnp.tile(jnp.arange(GRID_W, dtype=f32), rows)
    n_freq = rot_dim // 4
    inv = ROPE_BASE ** (-jnp.arange(n_freq, dtype=f32) / n_freq)
    ang = jnp.concatenate([row[:, None] * inv, col[:, None] * inv], -1)
    cos = jnp.repeat(jnp.cos(ang), 2, axis=1)
    sin = jnp.repeat(jnp.sin(ang), 2, axis=1)
    even = (jnp.arange(rot_dim) % 2 == 0)[None, :]
    sa = jnp.where(even, -sin, 0.0)
    sb = jnp.where(even, 0.0, sin)
    reps = LANES // rot_dim
    if reps >= 1:
        return tuple(jnp.tile(a, (1, reps)) for a in (cos, sa, sb))
    raise NotImplementedError


def _pad_kv_pairs(a):
    rows = a.shape[0]
    a = a.reshape(rows, SWA_KV, SWA_HD)
    z = jnp.zeros_like(a)
    return jnp.concatenate([a, z, z, a], axis=-1).reshape(rows, SWA_KV * 4 * SWA_HD)


def _split_hi_lo(w):
    hi = w.astype(bf16)
    return hi, (w - hi.astype(f32)).astype(bf16)


def kernel(x_prompt, x_sample, cache_mla_ckv, cache_mla_kpe, cache_swa_k, cache_swa_v, c, c_ctx, w_ada, b_ada, w_in_even, q_norm, kv_norm, w_uq, w_ukv, w_pool, pool_scale, w_out_even, w_in_odd, sink, w_out_odd, ln1_g, ln1_b, w_router, w_gate, w_up, w_down, ln2_g, ln2_b):
    depth, d = w_ada.shape[0], w_ada.shape[1]
    bp, sp, _ = x_prompt.shape
    bs, ss, _ = x_sample.shape
    past = cache_mla_ckv.shape[2]
    n_exp = w_router.shape[2]
    alpha = (2 * depth) ** 0.25
    q_lora, kv_lora = q_norm.shape[1], kv_norm.shape[1]
    pool_ch = pool_scale.shape[1]
    assert sp % ROW_TILE == 0 and ss % ROW_TILE == 0 and ss % GRID_W == 0
    assert QK_ROPE <= LANES and LANES % SWA_HD == 0 and 2 * SWA_HD == LANES

    n_rows = -(-(1 + bs) // 16) * 16
    cond = jnp.zeros((n_rows, d), f32).at[0].set(c_ctx).at[1:1 + bs].set(c)
    mods = _modulation(cond, w_ada, b_ada)
    mods = mods.reshape(depth, n_rows, 6, d)

    streams = {
        "p": dict(x=x_prompt.reshape(bp * sp, d), b=bp, s=sp, rope=False, mrow=slice(0, 1),
                  tiles_per_mod=bp * sp // ROW_TILE),
        "s": dict(x=x_sample.reshape(bs * ss, d), b=bs, s=ss, rope=True, mrow=slice(1, 1 + bs),
                  tiles_per_mod=ss // ROW_TILE),
    }
    for st in streams.values():
        st["cap"] = max(1, CAP_FACTOR * st["s"] // n_exp)
        st["nsb"] = math.gcd(st["b"], max(1, ROUTE_TOKENS // st["s"]))
        assert (n_exp * st["cap"]) % LANES == 0 and (LANES % st["cap"] == 0 or st["cap"] % LANES == 0)
        m_rows = st["b"] * st["cap"]
        assert m_rows % min(EXPERT_ROWS, m_rows) == 0 and m_rows % 16 == 0

    tri = jnp.triu(jnp.ones((LANES, LANES), bf16))
    rope_mla = _rope_tables(ss, QK_ROPE)
    rope_swa = _rope_tables(ss, SWA_HD)
    ident = (jnp.ones((ROW_TILE, LANES), f32), jnp.zeros((ROW_TILE, LANES), f32), jnp.zeros((ROW_TILE, LANES), f32))

    outs = {"ckv": [], "kpe": [], "k": [], "v": []}
    for l in range(depth):
        wr_hi, wr_lo = _split_hi_lo(jnp.pad(w_router[l], ((0, 0), (0, LANES - n_exp))))
        if l % 2 == 0:
            e = l // 2
            wi = w_in_even[e]
            o_pe = q_lora + kv_lora
            w_in = jnp.concatenate([wi[:, :o_pe + QK_ROPE], jnp.zeros((d, LANES - QK_ROPE), f32),
                                    wi[:, o_pe + QK_ROPE:]], axis=1).astype(bf16)
            wq = w_uq[e].reshape(q_lora, MLA_HEADS, QK_NOPE + QK_ROPE)
            wq = jnp.pad(wq, ((0, 0), (0, 0), (0, 2 * LANES - QK_NOPE - QK_ROPE)))
            wq = wq.reshape(q_lora, MLA_HEADS * 2 * LANES).astype(bf16)
            wkv = w_ukv[e].astype(bf16)
            w_out = w_out_even[e].astype(bf16)
            wp = w_pool[e].astype(bf16)
            kvc = _matmul(cache_mla_ckv[:, e].reshape(bs * past, kv_lora), wkv, bf16)
            kpec = jnp.pad(cache_mla_kpe[:, e].reshape(bs * past, QK_ROPE),
                           ((0, 0), (0, LANES - QK_ROPE))).astype(bf16)
        else:
            o = l // 2
            w_in = w_in_odd[o].astype(bf16)
            w_out = w_out_odd[o].astype(bf16)
            sink_b = jnp.broadcast_to(sink[o][:, None], (sink.shape[1], LANES))
            kc = _pad_kv_pairs(cache_swa_k[:, o].reshape(bs * past, SWA_KV * SWA_HD)).astype(bf16)
            vc = _pad_kv_pairs(cache_swa_v[:, o].reshape(bs * past, SWA_KV * SWA_HD)).astype(bf16)

        for name, st in streams.items():
            m = mods[l, st["mrow"]]
            tps = st["s"] // ROW_TILE
            common = dict(rope=st["rope"], tiles_per_mod=st["tiles_per_mod"], tiles_per_seq=tps)
            if l % 2 == 0:
                q, ckv, kv, kpe_raw, kpe_att, u = _in_even(
                    st["x"], m, w_in, q_norm[e][None], kv_norm[e][None], wq, wkv,
                    rope_mla if st["rope"] else ident, **common)
                pooled = _pool(u, wp, pool_scale[e][None], st["s"])
                ctx = (kvc, kpec, past) if st["rope"] else None
                attn = _mla_attention(q, kv, kpe_att, ctx, st["s"])
                pieces = [attn, pooled]
                if not st["rope"]:
                    outs["ckv"].append(ckv.reshape(st["b"], st["s"], kv_lora))
                    outs["kpe"].append(kpe_raw[:, :QK_ROPE].reshape(st["b"], st["s"], QK_ROPE))
            else:
                q, ka, va, kf, vf = _in_odd(st["x"], m, w_in, rope_swa if st["rope"] else ident,
                                            tqa=WINDOW if st["rope"] else st["s"], **common)
                ctx = (kc, vc, past) if st["rope"] else None
                attn = _gqa_attention(q, ka, va, sink_b, ctx, seq=st["s"], windowed=st["rope"])
                pieces = [attn]
                if not st["rope"]:
                    outs["k"].append(kf.reshape(st["b"], st["s"], SWA_KV, SWA_HD))
                    outs["v"].append(vf.reshape(st["b"], st["s"], SWA_KV, SWA_HD))
            st["x1"], h2, lg = _out_ln(pieces, w_out, st["x"], m, ln1_g[l][None], ln1_b[l][None],
                                       wr_hi, wr_lo, alpha=alpha, rows_per_mod=st["tiles_per_mod"] * ROW_TILE)
            st["xs"], st["gate"], st["pt"] = _route(lg, h2, tri, seq=st["s"], cap=st["cap"], nsb=st["nsb"],
                                                    n_exp=n_exp)

        yp, ys = _experts(streams["p"]["xs"], streams["s"]["xs"], streams["p"]["gate"], streams["s"]["gate"],
                          w_gate, w_up, w_down, l)
        for st, y in ((streams["p"], yp), (streams["s"], ys)):
            st["x"] = _combine(st["pt"], y, st["x1"], mods[l, st["mrow"]], ln2_g[l][None], ln2_b[l][None],
                               alpha=alpha, seq=st["s"], cap=st["cap"],
                               rows_per_mod=st["tiles_per_mod"] * ROW_TILE)

    y_prompt = streams["p"]["x"].reshape(bp, sp, d)
    y_sample = streams["s"]["x"].reshape(bs, ss, d)
    return (y_prompt, y_sample, jnp.stack(outs["ckv"], axis=1), jnp.stack(outs["kpe"], axis=1),
            jnp.stack(outs["k"], axis=1), jnp.stack(outs["v"], axis=1))
```

```python
import functools
import math

import numpy as np
import jax
import jax.numpy as jnp
from jax import lax
from jax.experimental import pallas as pl
from jax.experimental.pallas import tpu as pltpu

f32 = jnp.float32
bf16 = jnp.bfloat16

MLA_HEADS = 8
QK_NOPE = 128
QK_ROPE = 64
V_HEAD = 128
POOL_WINDOWS = (2, 4, 8, 16)
SWA_KV = 4
SWA_HD = 64
WINDOW = 128
CAP_FACTOR = 2
GRID_W = 64
ROPE_BASE = 10000.0
RMS_EPS = 1e-6
LN_EPS = 1e-5

LOG2E = 1.4426950408889634
LANES = 128
ROW_TILE = 256
MIB = 1024 * 1024


def _cparams(sem, vmem_mib):
    return pltpu.CompilerParams(dimension_semantics=sem, vmem_limit_bytes=vmem_mib * MIB)


def _dot(a, b):
    return jnp.dot(a, b, preferred_element_type=f32)


def _dot_nt(a, b):
    return lax.dot_general(a, b, (((1,), (1,)), ((), ())), preferred_element_type=f32)


def _silu(x):
    return x / (1.0 + jnp.exp(-x))


def _rope(r, c, sa, sb):
    return r * c + pltpu.roll(r, LANES - 1, axis=1) * sa + pltpu.roll(r, 1, axis=1) * sb


def _layer_norm(z, g, b):
    mu = jnp.mean(z, axis=-1, keepdims=True)
    zc = z - mu
    var = jnp.mean(zc * zc, axis=-1, keepdims=True)
    return zc * lax.rsqrt(var + LN_EPS) * g + b


def _mod_kernel(c_ref, w_ref, b_ref, o_ref):
    a = _silu(c_ref[...]).astype(bf16)
    o_ref[0] = _dot(a, w_ref[0].astype(bf16)) + b_ref[0]


def _modulation(cond, w_ada, b_ada):
    depth, d, n = w_ada.shape
    r = cond.shape[0]
    tn = 1024
    return pl.pallas_call(
        _mod_kernel,
        out_shape=jax.ShapeDtypeStruct((depth, r, n), f32),
        grid=(depth, n // tn),
        in_specs=[pl.BlockSpec((r, d), lambda l, j: (0, 0)),
                  pl.BlockSpec((1, d, tn), lambda l, j: (l, 0, j)),
                  pl.BlockSpec((1, 1, tn), lambda l, j: (l, 0, j))],
        out_specs=pl.BlockSpec((1, r, tn), lambda l, j: (l, 0, j)),
        compiler_params=_cparams(("arbitrary", "arbitrary"), 40),
        name="adaln_modulation",
    )(cond, w_ada, b_ada.reshape(depth, 1, n))


def _in_even_kernel(x_ref, mod_ref, win_ref, qn_ref, kvn_ref, wuq_ref, wukv_ref, rc_ref, rsa_ref, rsb_ref,
                    q_ref, ckv_ref, kv_ref, kper_ref, kpea_ref, u_ref, *, rope, q_lora, kv_lora, qscale):
    sh = mod_ref[0, 0:1, :]
    sc = 1.0 + mod_ref[0, 1:2, :]
    o_kv = q_lora
    o_pe = q_lora + kv_lora
    o_u = o_pe + LANES
    for r0 in range(0, x_ref.shape[0], ROW_TILE):
        rows = slice(r0, r0 + ROW_TILE)
        h = (x_ref[rows, :] * sc + sh).astype(bf16)
        proj = _dot(h, win_ref[...])
        pq = proj[:, 0:o_kv]
        pkv = proj[:, o_kv:o_pe]
        kpe = proj[:, o_pe:o_u]
        u_ref[rows, :] = proj[:, o_u:]
        cq = pq * lax.rsqrt(jnp.mean(pq * pq, axis=-1, keepdims=True) + RMS_EPS) * qn_ref[...]
        ckv = pkv * lax.rsqrt(jnp.mean(pkv * pkv, axis=-1, keepdims=True) + RMS_EPS) * kvn_ref[...]
        ckv_ref[rows, :] = ckv
        kv_ref[rows, :] = _dot(ckv.astype(bf16), wukv_ref[...]).astype(bf16)
        kper_ref[rows, :] = kpe
        q = _dot(cq.astype(bf16), wuq_ref[...]) * qscale
        if rope:
            c, sa, sb = rc_ref[rows, :], rsa_ref[rows, :], rsb_ref[rows, :]
            kpea_ref[rows, :] = _rope(kpe, c, sa, sb).astype(bf16)
        else:
            kpea_ref[rows, :] = kpe.astype(bf16)
        for hd in range(MLA_HEADS):
            a = 2 * hd * LANES
            q_ref[rows, a:a + LANES] = q[:, a:a + LANES].astype(bf16)
            r = q[:, a + LANES:a + 2 * LANES]
            if rope:
                r = _rope(r, c, sa, sb)
            q_ref[rows, a + LANES:a + 2 * LANES] = r.astype(bf16)


def _in_even(x, mods, w_in, qn, kvn, wuq, wukv, rtab, *, rope, rows_per_mod, seq):
    t, d = x.shape
    q_lora, kv_lora = qn.shape[1], kvn.shape[1]
    n_in = w_in.shape[1]
    n_u = n_in - q_lora - kv_lora - LANES
    nq, nkv = wuq.shape[1], wukv.shape[1]
    tm = ROW_TILE * SUB_TILES
    assert t % tm == 0 and rows_per_mod % tm == 0
    tiles_per_mod = rows_per_mod // tm
    tiles_per_seq = max(1, seq // tm)
    row = lambda i: (i, 0)
    const = lambda i: (0, 0)
    rmap = lambda i: (i % tiles_per_seq, 0)
    kern = functools.partial(_in_even_kernel, rope=rope, q_lora=q_lora, kv_lora=kv_lora,
                             qscale=(QK_NOPE + QK_ROPE) ** -0.5 * LOG2E)
    return pl.pallas_call(
        kern,
        out_shape=(jax.ShapeDtypeStruct((t, nq), bf16), jax.ShapeDtypeStruct((t, kv_lora), f32),
                   jax.ShapeDtypeStruct((t, nkv), bf16), jax.ShapeDtypeStruct((t, LANES), f32),
                   jax.ShapeDtypeStruct((t, LANES), bf16), jax.ShapeDtypeStruct((t, n_u), f32)),
        grid=(t // tm,),
        in_specs=[pl.BlockSpec((tm, d), row),
                  pl.BlockSpec((1, 6, d), lambda i: (i // tiles_per_mod, 0, 0)),
                  pl.BlockSpec((d, n_in), const),
                  pl.BlockSpec((1, q_lora), const), pl.BlockSpec((1, kv_lora), const),
                  pl.BlockSpec((q_lora, nq), const), pl.BlockSpec((kv_lora, nkv), const),
                  pl.BlockSpec((tm, LANES), rmap), pl.BlockSpec((tm, LANES), rmap), pl.BlockSpec((tm, LANES), rmap)],
        out_specs=(pl.BlockSpec((tm, nq), row), pl.BlockSpec((tm, kv_lora), row), pl.BlockSpec((tm, nkv), row),
                   pl.BlockSpec((tm, LANES), row), pl.BlockSpec((tm, LANES), row), pl.BlockSpec((tm, n_u), row)),
        compiler_params=_cparams(("arbitrary",), 56),
        name="in_proj_even",
    )(x, mods, w_in, qn, kvn, wuq, wukv, *rtab)


def _mm_kernel(x_ref, w_ref, o_ref):
    o_ref[...] = _dot(x_ref[...].astype(bf16), w_ref[...]).astype(o_ref.dtype)


def _matmul(x, w, out_dtype, tm=512):
    m, k = x.shape
    n = w.shape[1]
    tm = min(tm, m)
    assert m % tm == 0
    return pl.pallas_call(
        _mm_kernel,
        out_shape=jax.ShapeDtypeStruct((m, n), out_dtype),
        grid=(m // tm,),
        in_specs=[pl.BlockSpec((tm, k), lambda i: (i, 0)), pl.BlockSpec((k, n), lambda i: (0, 0))],
        out_specs=pl.BlockSpec((tm, n), lambda i: (i, 0)),
        compiler_params=_cparams(("arbitrary",), 24),
        name="ctx_kv_up",
    )(x, w)


def _pool_kernel(u_ref, wp_ref, ps_ref, o_ref, *, seq):
    gc = wp_ref.shape[1]
    t = lax.broadcasted_iota(jnp.int32, (seq, 1), 0)

    def down(a, k):
        return jnp.where(t >= k, pltpu.roll(a, k, axis=0), 0.0)

    def up(a, k):
        return jnp.where(t < seq - k, pltpu.roll(a, seq - k, axis=0), 0.0)

    for gi, w in enumerate(POOL_WINDOWS):
        half = w // 2
        a = u_ref[:, gi * gc:(gi + 1) * gc]
        trail, lead, k = a, a, 1
        while k < half:
            trail = trail + down(trail, k)
            lead = lead + up(lead, k)
            k *= 2
        win = down(trail, 1) + lead
        cnt = (jnp.minimum(t + half, seq) - jnp.maximum(t - half, 0)).astype(f32)
        pooled = win / cnt - a
        mixed = _dot(pooled.astype(bf16), wp_ref[gi])
        o_ref[:, gi * gc:(gi + 1) * gc] = (mixed * ps_ref[:, gi * gc:(gi + 1) * gc]).astype(bf16)


def _pool(u, w_pool, pool_scale, seq):
    t, ch = u.shape
    g, gc, _ = w_pool.shape
    return pl.pallas_call(
        functools.partial(_pool_kernel, seq=seq),
        out_shape=jax.ShapeDtypeStruct((t, ch), bf16),
        grid=(t // seq,),
        in_specs=[pl.BlockSpec((seq, ch), lambda b: (b, 0)),
                  pl.BlockSpec((g, gc, gc), lambda b: (0, 0, 0)),
                  pl.BlockSpec((1, ch), lambda b: (0, 0))],
        out_specs=pl.BlockSpec((seq, ch), lambda b: (b, 0)),
        compiler_params=_cparams(("arbitrary",), 40),
        name="pool_mixer",
    )(u, w_pool, pool_scale)


def _softmax_parts(scores):
    m = scores[0].max(axis=-1, keepdims=True)
    for s in scores[1:]:
        m = jnp.maximum(m, s.max(axis=-1, keepdims=True))
    ps = [jnp.exp2(s - m) for s in scores]
    l = ps[0].sum(axis=-1, keepdims=True)
    for p in ps[1:]:
        l = l + p.sum(axis=-1, keepdims=True)
    return ps, l, m


def _mla_kernel(*refs, has_ctx):
    if has_ctx:
        q_ref, kv_ref, kpe_ref, kvc_ref, kpec_ref, o_ref = refs
    else:
        q_ref, kv_ref, kpe_ref, o_ref = refs
    kpe = kpe_ref[...]
    for hd in range(MLA_HEADS):
        a = 2 * hd * LANES
        q = q_ref[:, a:a + 2 * LANES]
        k = jnp.concatenate([kv_ref[:, a:a + LANES], kpe], axis=1)
        scores = [_dot_nt(q, k)]
        vals = [kv_ref[:, a + LANES:a + 2 * LANES]]
        if has_ctx:
            kc = jnp.concatenate([kvc_ref[:, a:a + LANES], kpec_ref[...]], axis=1)
            scores.append(_dot_nt(q, kc))
            vals.append(kvc_ref[:, a + LANES:a + 2 * LANES])
        ps, l, _ = _softmax_parts(scores)
        o = _dot(ps[0].astype(bf16), vals[0])
        for p, v in zip(ps[1:], vals[1:]):
            o = o + _dot(p.astype(bf16), v)
        o_ref[:, hd * LANES:(hd + 1) * LANES] = (o / l).astype(bf16)


def _mla_attention(q, kv, kpe, ctx, seq):
    t, nq = q.shape
    tq = ROW_TILE
    tps = seq // tq
    in_specs = [pl.BlockSpec((tq, nq), lambda b, j: (b * tps + j, 0)),
                pl.BlockSpec((seq, nq), lambda b, j: (b, 0)),
                pl.BlockSpec((seq, LANES), lambda b, j: (b, 0))]
    args = [q, kv, kpe]
    if ctx is not None:
        kvc, kpec, past = ctx
        in_specs += [pl.BlockSpec((past, nq), lambda b, j: (b, 0)),
                     pl.BlockSpec((past, LANES), lambda b, j: (b, 0))]
        args += [kvc, kpec]
    return pl.pallas_call(
        functools.partial(_mla_kernel, has_ctx=ctx is not None),
        out_shape=jax.ShapeDtypeStruct((t, MLA_HEADS * V_HEAD), bf16),
        grid=(t // seq, tps),
        in_specs=in_specs,
        out_specs=pl.BlockSpec((tq, MLA_HEADS * V_HEAD), lambda b, j: (b * tps + j, 0)),
        compiler_params=_cparams(("arbitrary", "arbitrary"), 48),
        name="mla_attention",
    )(*args)


SUB_TILES = 2


def _out_ln_kernel(*refs, n_pieces, alpha):
    pieces = refs[:n_pieces]
    w_ref, x_ref, mod_ref, g_ref, b_ref, wr_ref, x1_ref, h2_ref, lg_ref = refs[n_pieces:]
    g1 = mod_ref[0, 2:3, :]
    sc2 = 1.0 + mod_ref[0, 4:5, :]
    sh2 = mod_ref[0, 3:4, :]
    for r in range(0, x_ref.shape[0], ROW_TILE):
        rows = slice(r, r + ROW_TILE)
        off = 0
        mix = None
        for p in pieces:
            kp = p.shape[1]
            part = _dot(p[rows, :], w_ref[off:off + kp, :])
            mix = part if mix is None else mix + part
            off += kp
        x1 = _layer_norm(alpha * x_ref[rows, :] + g1 * mix, g_ref[...], b_ref[...])
        x1_ref[rows, :] = x1
        h2 = x1 * sc2 + sh2
        h2h = h2.astype(bf16)
        h2_ref[rows, :] = h2h
        h2l = (h2 - h2h.astype(f32)).astype(bf16)
        hh = _dot(h2h, wr_ref[...])
        lg_ref[rows, :] = hh[:, 0:LANES] + hh[:, LANES:] + _dot(h2l, wr_ref[:, 0:LANES])


def _out_ln(pieces, w_out, x, mods, ln_g, ln_b, wr, *, alpha, rows_per_mod):
    t, d = x.shape
    tm = ROW_TILE * SUB_TILES
    assert t % tm == 0 and rows_per_mod % tm == 0
    row = lambda i: (i, 0)
    const = lambda i: (0, 0)
    in_specs = [pl.BlockSpec((tm, p.shape[1]), row) for p in pieces]
    in_specs += [pl.BlockSpec(w_out.shape, const), pl.BlockSpec((tm, d), row),
                 pl.BlockSpec((1, 6, d), lambda i: (i // (rows_per_mod // tm), 0, 0)),
                 pl.BlockSpec((1, d), const), pl.BlockSpec((1, d), const),
                 pl.BlockSpec((d, 2 * LANES), const)]
    return pl.pallas_call(
        functools.partial(_out_ln_kernel, n_pieces=len(pieces), alpha=alpha),
        out_shape=(jax.ShapeDtypeStruct((t, d), f32), jax.ShapeDtypeStruct((t, d), bf16),
                   jax.ShapeDtypeStruct((t, LANES), f32)),
        grid=(t // tm,),
        in_specs=in_specs,
        out_specs=(pl.BlockSpec((tm, d), row), pl.BlockSpec((tm, d), row), pl.BlockSpec((tm, LANES), row)),
        compiler_params=_cparams(("arbitrary",), 56),
        name="out_proj_ln",
    )(*pieces, w_out, x, mods, ln_g, ln_b, wr)


SEARCH_BITS = 4
ROUTE_TOKENS = 1024
DISPATCH_ROWS = 512


def _kth_largest_bits(bits, cap):
    thr = jnp.zeros((bits.shape[0], 1), jnp.int32)
    hi = 31
    while hi > 0:
        lo = max(hi - SEARCH_BITS, 0) if hi % SEARCH_BITS == 0 else hi - hi % SEARCH_BITS
        digit = jnp.zeros_like(thr)
        for c in range(1, 1 << (hi - lo)):
            cnt = jnp.sum(jnp.where(bits >= (thr | (c << lo)), 1.0, 0.0), axis=1, keepdims=True)
            digit = digit + jnp.where(cnt >= cap, 1, 0)
        thr = thr | (digit * (1 << lo))
        hi = lo
    return thr


def _lane_prefix(mask, tri):
    n_rows, n = mask.shape
    blocks = [mask[:, k:k + LANES] for k in range(0, n, LANES)]
    local = _dot(jnp.concatenate(blocks, axis=0).astype(bf16), tri)
    out, carry = [], None
    for i in range(len(blocks)):
        blk = local[i * n_rows:(i + 1) * n_rows, :]
        out.append(blk if carry is None else blk + carry)
        total = blk[:, LANES - 1:LANES]
        carry = total if carry is None else carry + total
    return jnp.concatenate(out, axis=1)


def _route_kernel(lg_ref, h2_ref, tri_ref, xs_ref, gate_ref, pt_ref, aff_sc, pos_sc, *, cap, seq, n_exp, nsb):
    tri = tri_ref[...]
    pos_sc[...] = jnp.full(pos_sc.shape, -1.0, f32)
    for sb in range(nsb):
        lg = lg_ref[sb * seq:(sb + 1) * seq, :].T[0:n_exp, :]
        z = jnp.exp(lg - lg.max(axis=0, keepdims=True))
        aff = z / z.sum(axis=0, keepdims=True)
        bits = pltpu.bitcast(aff, jnp.int32)
        thr = _kth_largest_bits(bits, cap)
        gt = bits > thr
        eq = bits == thr
        need = cap - jnp.sum(jnp.where(gt, 1.0, 0.0), axis=1, keepdims=True)
        eq_rank = _lane_prefix(jnp.where(eq, 1.0, 0.0), tri)
        sel = gt | (eq & (eq_rank <= need))
        sel_rank = _lane_prefix(jnp.where(sel, 1.0, 0.0), tri)
        aff_sc[sb] = aff
        pos_sc[sb, 0:n_exp, :] = jnp.where(sel, sel_rank - 1.0, -1.0)

    slot_col = lax.broadcasted_iota(jnp.int32, (cap, seq), 0).astype(f32)

    eg = max(1, min(n_exp, DISPATCH_ROWS // cap))

    def dispatch(g, carry):
        for sb in range(nsb):
            rows = slice(sb * cap, (sb + 1) * cap)
            onehot = []
            for k in range(eg):
                e = g * eg + k
                hit = pos_sc[sb, pl.ds(e, 1), :] == slot_col
                onehot.append(jnp.where(hit, 1.0, 0.0).astype(bf16))
                gate = jnp.sum(jnp.where(hit, aff_sc[sb, pl.ds(e, 1), :], 0.0), axis=1, keepdims=True)
                gate_ref[e, rows, :] = jnp.broadcast_to(gate, (cap, LANES))
            picked = _dot(jnp.concatenate(onehot, axis=0), h2_ref[sb * seq:(sb + 1) * seq, :])
            for k in range(eg):
                xs_ref[g * eg + k, rows, :] = picked[k * cap:(k + 1) * cap, :].astype(bf16)
        return carry

    lax.fori_loop(0, n_exp // eg, dispatch, 0)

    lane = lax.broadcasted_iota(jnp.int32, (1, LANES), 1)
    for sb in range(nsb):
        pos_t = pos_sc[sb].T
        for j in range(n_exp * cap // LANES):
            if cap >= LANES:
                e = j * LANES // cap
                target = (lane + (j * LANES - e * cap)).astype(f32)
                hit = pos_t[:, e:e + 1] == target
            else:
                per = LANES // cap
                hit = None
                for k in range(per):
                    e = j * per + k
                    target = jnp.where((lane >= k * cap) & (lane < (k + 1) * cap), lane - k * cap, -7).astype(f32)
                    m = pos_t[:, e:e + 1] == target
                    hit = m if hit is None else hit | m
            pt_ref[sb * seq:(sb + 1) * seq, j * LANES:(j + 1) * LANES] = jnp.where(hit, 1.0, 0.0).astype(bf16)


def _route(lg, h2, tri, *, seq, cap, nsb, n_exp):
    t, d = h2.shape
    nb = t // seq
    assert nb % nsb == 0
    return pl.pallas_call(
        functools.partial(_route_kernel, cap=cap, seq=seq, n_exp=n_exp, nsb=nsb),
        out_shape=(jax.ShapeDtypeStruct((n_exp, nb * cap, d), bf16),
                   jax.ShapeDtypeStruct((n_exp, nb * cap, LANES), f32),
                   jax.ShapeDtypeStruct((t, n_exp * cap), bf16)),
        grid=(nb // nsb,),
        in_specs=[pl.BlockSpec((nsb * seq, LANES), lambda b: (b, 0)),
                  pl.BlockSpec((nsb * seq, d), lambda b: (b, 0)),
                  pl.BlockSpec((LANES, LANES), lambda b: (0, 0))],
        out_specs=(pl.BlockSpec((n_exp, nsb * cap, d), lambda b: (0, b, 0)),
                   pl.BlockSpec((n_exp, nsb * cap, LANES), lambda b: (0, b, 0)),
                   pl.BlockSpec((nsb * seq, n_exp * cap), lambda b: (b, 0))),
        scratch_shapes=[pltpu.VMEM((nsb, n_exp, seq), f32), pltpu.VMEM((nsb, LANES, seq), f32)],
        compiler_params=_cparams(("arbitrary",), 56),
        name="route_dispatch",
    )(lg, h2, tri)


EXPERT_TILE = 512
EXPERT_ROWS = 256


def _expert_kernel(xp_ref, xs_ref, gp_ref, gs_ref, wg_ref, wu_ref, wd_ref, yp_ref, ys_ref,
                   hid_sc, wa_sc, wb_sc, wc_sc, *, nf, tf):
    j = pl.program_id(1)
    mp, ms = xp_ref.shape[1], xs_ref.shape[1]
    cp, cs = min(EXPERT_ROWS, mp), min(EXPERT_ROWS, ms)
    chunks = [(xp_ref, gp_ref, yp_ref, r, r, cp) for r in range(0, mp, cp)]
    chunks += [(xs_ref, gs_ref, ys_ref, r, mp + r, cs) for r in range(0, ms, cs)]

    @pl.when(j < nf)
    def _():
        wa_sc[...] = wg_ref[0, 0].astype(bf16)
        wb_sc[...] = wu_ref[0, 0].astype(bf16)
        for x_ref, _, _, r, m, n in chunks:
            xc = x_ref[0, r:r + n, :]
            g = _dot(xc, wa_sc[...])
            u = _dot(xc, wb_sc[...])
            hid_sc[j, m:m + n, :] = (_silu(g) * u).astype(bf16)

    @pl.when(j >= nf)
    def _():
        wc_sc[...] = wd_ref[0, 0].astype(bf16)
        for _, g_ref, y_ref, r, m, n in chunks:
            acc = _dot(hid_sc[0, m:m + n, :], wc_sc[0:tf, :])
            for f in range(1, nf):
                acc = acc + _dot(hid_sc[f, m:m + n, :], wc_sc[f * tf:(f + 1) * tf, :])
            y_ref[0, r:r + n, :] = (acc * g_ref[0, r:r + n, 0:1]).astype(bf16)


def _experts(xp, xs, gp, gs, w_gate, w_up, w_down, layer):
    n_exp, mp, d = xp.shape
    ms = xs.shape[1]
    ff = w_gate.shape[3]
    tf = tn = EXPERT_TILE
    nf, nn = ff // tf, d // tn
    x_map = lambda e, j: (e, 0, 0)
    up_map = lambda e, j: (layer, e, 0, jnp.minimum(j, nf - 1))
    dn_map = lambda e, j: (layer, e, 0, jnp.maximum(j - nf, 0))
    y_map = lambda e, j: (e, 0, jnp.maximum(j - nf, 0))
    return pl.pallas_call(
        functools.partial(_expert_kernel, nf=nf, tf=tf),
        out_shape=(jax.ShapeDtypeStruct((n_exp, mp, d), bf16), jax.ShapeDtypeStruct((n_exp, ms, d), bf16)),
        grid=(n_exp, nf + nn),
        in_specs=[pl.BlockSpec((1, mp, d), x_map), pl.BlockSpec((1, ms, d), x_map),
                  pl.BlockSpec((1, mp, LANES), x_map), pl.BlockSpec((1, ms, LANES), x_map),
                  pl.BlockSpec((1, 1, d, tf), up_map), pl.BlockSpec((1, 1, d, tf), up_map),
                  pl.BlockSpec((1, 1, ff, tn), dn_map)],
        out_specs=(pl.BlockSpec((1, mp, tn), y_map), pl.BlockSpec((1, ms, tn), y_map)),
        scratch_shapes=[pltpu.VMEM((nf, mp + ms, tf), bf16), pltpu.VMEM((d, tf), bf16),
                        pltpu.VMEM((d, tf), bf16), pltpu.VMEM((ff, tn), bf16)],
        compiler_params=_cparams(("arbitrary", "arbitrary"), 58),
        name="expert_swiglu",
    )(xp, xs, gp, gs, w_gate, w_up, w_down)


def _combine_kernel(pt_ref, y_ref, x_ref, mod_ref, g_ref, b_ref, o_ref, *, alpha, seq, cap):
    n_exp, _, d = y_ref.shape
    g2 = mod_ref[0, 5:6, :]
    for r in range(0, x_ref.shape[0], ROW_TILE):
        rows = slice(r, r + ROW_TILE)
        sq = r // seq
        y = y_ref[:, sq * cap:(sq + 1) * cap, :].reshape(n_exp * cap, d)
        ffn = _dot(pt_ref[rows, :], y)
        o_ref[rows, :] = _layer_norm(alpha * x_ref[rows, :] + g2 * ffn, g_ref[...], b_ref[...])


def _combine(pt, y, x1, mods, ln_g, ln_b, *, alpha, seq, cap, rows_per_mod):
    t, d = x1.shape
    n_exp = y.shape[0]
    tm = ROW_TILE * SUB_TILES
    assert t % tm == 0 and rows_per_mod % tm == 0 and (tm % seq == 0 or seq % tm == 0)
    spt = max(1, tm // seq)
    tps = max(1, seq // tm)
    row = lambda i: (i, 0)
    const = lambda i: (0, 0)
    return pl.pallas_call(
        functools.partial(_combine_kernel, alpha=alpha, seq=seq, cap=cap),
        out_shape=jax.ShapeDtypeStruct((t, d), f32),
        grid=(t // tm,),
        in_specs=[pl.BlockSpec((tm, n_exp * cap), row),
                  pl.BlockSpec((n_exp, spt * cap, d), lambda i: (0, i // tps, 0)),
                  pl.BlockSpec((tm, d), row),
                  pl.BlockSpec((1, 6, d), lambda i: (i // (rows_per_mod // tm), 0, 0)),
                  pl.BlockSpec((1, d), const), pl.BlockSpec((1, d), const)],
        out_specs=pl.BlockSpec((tm, d), row),
        compiler_params=_cparams(("arbitrary",), 56),
        name="combine_ln",
    )(pt, y, x1, mods, ln_g, ln_b)


def _pad_pairs(chunk, lane):
    swapped = pltpu.roll(chunk, LANES // 2, axis=1)
    lo = lane < LANES // 2
    return [jnp.where(lo, chunk, 0.0), jnp.where(lo, 0.0, swapped),
            jnp.where(lo, swapped, 0.0), jnp.where(lo, 0.0, chunk)]


def _in_odd_kernel(x_ref, mod_ref, win_ref, rc_ref, rsa_ref, rsb_ref,
                   q_ref, ka_ref, va_ref, kf_ref, vf_ref, *, rope, nq, nkv, qscale, tqa):
    sh = mod_ref[0, 0:1, :]
    sc = 1.0 + mod_ref[0, 1:2, :]
    pairs = nq // LANES // SWA_KV
    lane = lax.broadcasted_iota(jnp.int32, (1, LANES), 1)
    for r0 in range(0, x_ref.shape[0], ROW_TILE):
        rows = slice(r0, r0 + ROW_TILE)
        h = (x_ref[rows, :] * sc + sh).astype(bf16)
        proj = _dot(h, win_ref[...])
        k = proj[:, nq:nq + nkv]
        v = proj[:, nq + nkv:nq + 2 * nkv]
        kf_ref[rows, :] = k
        vf_ref[rows, :] = v
        if rope:
            c, sa, sb = rc_ref[rows, :], rsa_ref[rows, :], rsb_ref[rows, :]
        for ci in range(nq // LANES):
            r = proj[:, ci * LANES:(ci + 1) * LANES]
            if rope:
                r = _rope(r, c, sa, sb)
            r = (r * qscale).astype(bf16)
            hk, pr = ci // pairs, ci % pairs
            for tb in range(ROW_TILE // tqa):
                q_ref[r0 // tqa + tb, hk, pr * tqa:(pr + 1) * tqa, :] = r[tb * tqa:(tb + 1) * tqa, :]
        for a in range(0, nkv, LANES):
            kc = k[:, a:a + LANES]
            if rope:
                kc = _rope(kc, c, sa, sb)
            for i, (kp, vp) in enumerate(zip(_pad_pairs(kc, lane), _pad_pairs(v[:, a:a + LANES], lane))):
                o = 4 * a + i * LANES
                ka_ref[rows, o:o + LANES] = kp.astype(bf16)
                va_ref[rows, o:o + LANES] = vp.astype(bf16)


def _in_odd(x, mods, w_in, rtab, *, rope, tqa, rows_per_mod, seq):
    t, d = x.shape
    n_in = w_in.shape[1]
    nkv = SWA_KV * SWA_HD
    nq = n_in - 2 * nkv
    tm = ROW_TILE * SUB_TILES
    assert ROW_TILE % tqa == 0 and t % tm == 0 and rows_per_mod % tm == 0
    tiles_per_mod = rows_per_mod // tm
    tiles_per_seq = max(1, seq // tm)
    q_rows = nq // LANES // SWA_KV * tqa
    row = lambda i: (i, 0)
    rmap = lambda i: (i % tiles_per_seq, 0)
    kern = functools.partial(_in_odd_kernel, rope=rope, nq=nq, nkv=nkv, qscale=SWA_HD ** -0.5 * LOG2E, tqa=tqa)
    return pl.pallas_call(
        kern,
        out_shape=(jax.ShapeDtypeStruct((t // tqa, SWA_KV, q_rows, LANES), bf16),
                   jax.ShapeDtypeStruct((t, 4 * nkv), bf16),
                   jax.ShapeDtypeStruct((t, 4 * nkv), bf16), jax.ShapeDtypeStruct((t, nkv), f32),
                   jax.ShapeDtypeStruct((t, nkv), f32)),
        grid=(t // tm,),
        in_specs=[pl.BlockSpec((tm, d), row),
                  pl.BlockSpec((1, 6, d), lambda i: (i // tiles_per_mod, 0, 0)),
                  pl.BlockSpec((d, n_in), lambda i: (0, 0)),
                  pl.BlockSpec((tm, LANES), rmap), pl.BlockSpec((tm, LANES), rmap), pl.BlockSpec((tm, LANES), rmap)],
        out_specs=(pl.BlockSpec((tm // tqa, SWA_KV, q_rows, LANES), lambda i: (i, 0, 0, 0)),
                   pl.BlockSpec((tm, 4 * nkv), row), pl.BlockSpec((tm, 4 * nkv), row),
                   pl.BlockSpec((tm, nkv), row), pl.BlockSpec((tm, nkv), row)),
        compiler_params=_cparams(("arbitrary",), 56),
        name="in_proj_odd",
    )(x, mods, w_in, *rtab)


def _gqa_kernel(*refs, windowed, seq, tq):
    if windowed:
        q_ref, ka_ref, va_ref, kc_ref, vc_ref, sink_ref, o_ref = refs
    else:
        q_ref, ka_ref, va_ref, sink_ref, o_ref = refs
    pairs = q_ref.shape[2] // tq
    if windowed:
        span = tq + 2 * WINDOW
        q0 = pl.program_id(1) * tq
        start = pl.multiple_of(jnp.clip(q0 - WINDOW, 0, seq - span), LANES)
        kpos = start + lax.broadcasted_iota(jnp.int32, (tq, span), 1)
        qpos = q0 + lax.broadcasted_iota(jnp.int32, (tq, span), 0)
        bias1 = jnp.where(jnp.abs(kpos - qpos) <= WINDOW, 0.0, -1e30)
        bias = jnp.concatenate([bias1] * pairs, axis=0)
        rows = pl.ds(start, span)
    else:
        rows = slice(None)
    for hk in range(SWA_KV):
        q = q_ref[0, hk]
        halves = []
        for s in range(2):
            cs = slice((2 * hk + s) * LANES, (2 * hk + s + 1) * LANES)
            heads = [2 * (hk * pairs + pr) + s for pr in range(pairs)]
            sink = jnp.concatenate([jnp.broadcast_to(sink_ref[h:h + 1, 0:1], (tq, 1)) for h in heads],
                                   axis=0) * LOG2E
            sc = _dot_nt(q, ka_ref[rows, cs])
            if windowed:
                scores = [sc + bias, _dot_nt(q, kc_ref[:, cs])]
                vals = [va_ref[rows, cs], vc_ref[:, cs]]
            else:
                scores, vals = [sc], [va_ref[rows, cs]]
            m = sink
            for x in scores:
                m = jnp.maximum(m, x.max(axis=-1, keepdims=True))
            l = jnp.exp2(sink - m)
            o = None
            for x, v in zip(scores, vals):
                p = jnp.exp2(x - m)
                l = l + p.sum(axis=-1, keepdims=True)
                pv = _dot(p.astype(bf16), v)
                o = pv if o is None else o + pv
            halves.append(o / l)
        out = halves[0] + halves[1]
        for pr in range(pairs):
            col = (hk * pairs + pr) * LANES
            o_ref[:, col:col + LANES] = out[pr * tq:(pr + 1) * tq, :].astype(bf16)


def _gqa_attention(q, ka, va, sink, ctx, *, seq, windowed):
    t, nk = ka.shape
    tq = WINDOW if windowed else seq
    tps = seq // tq
    q_rows = q.shape[2]
    nq = q_rows // tq * SWA_KV * LANES
    in_specs = [pl.BlockSpec((1, SWA_KV, q_rows, LANES), lambda b, j: (b * tps + j, 0, 0, 0)),
                pl.BlockSpec((seq, nk), lambda b, j: (b, 0)),
                pl.BlockSpec((seq, nk), lambda b, j: (b, 0))]
    args = [q, ka, va]
    if windowed:
        kc, vc, past = ctx
        in_specs += [pl.BlockSpec((past, nk), lambda b, j: (b, 0)), pl.BlockSpec((past, nk), lambda b, j: (b, 0))]
        args += [kc, vc]
    in_specs.append(pl.BlockSpec(sink.shape, lambda b, j: (0, 0)))
    args.append(sink)
    return pl.pallas_call(
        functools.partial(_gqa_kernel, windowed=windowed, seq=seq, tq=tq),
        out_shape=jax.ShapeDtypeStruct((t, nq), bf16),
        grid=(t // seq, tps),
        in_specs=in_specs,
        out_specs=pl.BlockSpec((tq, nq), lambda b, j: (b * tps + j, 0)),
        compiler_params=_cparams(("arbitrary", "arbitrary"), 40),
        name="gqa_attention",
    )(*args)


def _rope_tables(n_tokens, rot_dim):
    rows = n_tokens // GRID_W
    row = jnp.repeat(jnp.arange(rows, dtype=f32), GRID_W)
    col = jnp.tile(jnp.arange(GRID_W, dtype=f32), rows)
    n_freq = rot_dim // 4
    inv = ROPE_BASE ** (-jnp.arange(n_freq, dtype=f32) / n_freq)
    ang = jnp.concatenate([row[:, None] * inv, col[:, None] * inv], -1)
    cos = jnp.repeat(jnp.cos(ang), 2, axis=1)
    sin = jnp.repeat(jnp.sin(ang), 2, axis=1)
    even = (jnp.arange(rot_dim) % 2 == 0)[None, :]
    sa = jnp.where(even, -sin, 0.0)
    sb = jnp.where(even, 0.0, sin)
    reps = LANES // rot_dim
    if reps >= 1:
        return tuple(jnp.tile(a, (1, reps)) for a in (cos, sa, sb))
    raise NotImplementedError


def _pad_kv_pairs(a):
    rows = a.shape[0]
    a = a.reshape(rows, SWA_KV, SWA_HD)
    z = jnp.zeros_like(a)
    return jnp.concatenate([a, z, z, a], axis=-1).reshape(rows, SWA_KV * 4 * SWA_HD)


def _split_hi_lo(w):
    hi = w.astype(bf16)
    return hi, (w - hi.astype(f32)).astype(bf16)


def kernel(x_prompt, x_sample, cache_mla_ckv, cache_mla_kpe, cache_swa_k, cache_swa_v, c, c_ctx, w_ada, b_ada, w_in_even, q_norm, kv_norm, w_uq, w_ukv, w_pool, pool_scale, w_out_even, w_in_odd, sink, w_out_odd, ln1_g, ln1_b, w_router, w_gate, w_up, w_down, ln2_g, ln2_b):
    depth, d = w_ada.shape[0], w_ada.shape[1]
    bp, sp, _ = x_prompt.shape
    bs, ss, _ = x_sample.shape
    past = cache_mla_ckv.shape[2]
    n_exp = w_router.shape[2]
    alpha = (2 * depth) ** 0.25
    q_lora, kv_lora = q_norm.shape[1], kv_norm.shape[1]
    pool_ch = pool_scale.shape[1]
    assert sp % ROW_TILE == 0 and ss % ROW_TILE == 0 and ss % GRID_W == 0
    assert QK_ROPE <= LANES and LANES % SWA_HD == 0 and 2 * SWA_HD == LANES

    n_rows = -(-(1 + bs) // 16) * 16
    cond = jnp.zeros((n_rows, d), f32).at[0].set(c_ctx).at[1:1 + bs].set(c)
    mods = _modulation(cond, w_ada, b_ada)
    mods = mods.reshape(depth, n_rows, 6, d)

    streams = {
        "p": dict(x=x_prompt.reshape(bp * sp, d), b=bp, s=sp, rope=False, mrow=slice(0, 1),
                  tiles_per_mod=bp * sp // ROW_TILE),
        "s": dict(x=x_sample.reshape(bs * ss, d), b=bs, s=ss, rope=True, mrow=slice(1, 1 + bs),
                  tiles_per_mod=ss // ROW_TILE),
    }
    for st in streams.values():
        st["cap"] = max(1, CAP_FACTOR * st["s"] // n_exp)
        st["nsb"] = math.gcd(st["b"], max(1, ROUTE_TOKENS // st["s"]))
        assert (n_exp * st["cap"]) % LANES == 0 and (LANES % st["cap"] == 0 or st["cap"] % LANES == 0)
        m_rows = st["b"] * st["cap"]
        assert m_rows % min(EXPERT_ROWS, m_rows) == 0 and m_rows % 16 == 0

    tri = jnp.triu(jnp.ones((LANES, LANES), bf16))
    rope_mla = _rope_tables(ss, QK_ROPE)
    rope_swa = _rope_tables(ss, SWA_HD)
    ident = (jnp.ones((ROW_TILE * SUB_TILES, LANES), f32),) + (jnp.zeros((ROW_TILE * SUB_TILES, LANES), f32),) * 2

    outs = {"ckv": [], "kpe": [], "k": [], "v": []}
    for l in range(depth):
        wr = jnp.concatenate(_split_hi_lo(jnp.pad(w_router[l], ((0, 0), (0, LANES - n_exp)))), axis=1)
        if l % 2 == 0:
            e = l // 2
            wi = w_in_even[e]
            o_pe = q_lora + kv_lora
            w_in = jnp.concatenate([wi[:, :o_pe + QK_ROPE], jnp.zeros((d, LANES - QK_ROPE), f32),
                                    wi[:, o_pe + QK_ROPE:]], axis=1).astype(bf16)
            wq = w_uq[e].reshape(q_lora, MLA_HEADS, QK_NOPE + QK_ROPE)
            wq = jnp.pad(wq, ((0, 0), (0, 0), (0, 2 * LANES - QK_NOPE - QK_ROPE)))
            wq = wq.reshape(q_lora, MLA_HEADS * 2 * LANES).astype(bf16)
            wkv = w_ukv[e].astype(bf16)
            w_out = w_out_even[e].astype(bf16)
            wp = w_pool[e].astype(bf16)
            kvc = _matmul(cache_mla_ckv[:, e].reshape(bs * past, kv_lora), wkv, bf16)
            kpec = jnp.pad(cache_mla_kpe[:, e].reshape(bs * past, QK_ROPE),
                           ((0, 0), (0, LANES - QK_ROPE))).astype(bf16)
        else:
            o = l // 2
            w_in = w_in_odd[o].astype(bf16)
            w_out = w_out_odd[o].astype(bf16)
            sink_b = jnp.broadcast_to(sink[o][:, None], (sink.shape[1], LANES))
            kc = _pad_kv_pairs(cache_swa_k[:, o].reshape(bs * past, SWA_KV * SWA_HD)).astype(bf16)
            vc = _pad_kv_pairs(cache_swa_v[:, o].reshape(bs * past, SWA_KV * SWA_HD)).astype(bf16)

        for name, st in streams.items():
            m = mods[l, st["mrow"]]
            common = dict(rope=st["rope"], rows_per_mod=st["tiles_per_mod"] * ROW_TILE, seq=st["s"])
            if l % 2 == 0:
                q, ckv, kv, kpe_raw, kpe_att, u = _in_even(
                    st["x"], m, w_in, q_norm[e][None], kv_norm[e][None], wq, wkv,
                    rope_mla if st["rope"] else ident, **common)
                pooled = _pool(u, wp, pool_scale[e][None], st["s"])
                ctx = (kvc, kpec, past) if st["rope"] else None
                attn = _mla_attention(q, kv, kpe_att, ctx, st["s"])
                pieces = [attn, pooled]
                if not st["rope"]:
                    outs["ckv"].append(ckv.reshape(st["b"], st["s"], kv_lora))
                    outs["kpe"].append(kpe_raw[:, :QK_ROPE].reshape(st["b"], st["s"], QK_ROPE))
            else:
                q, ka, va, kf, vf = _in_odd(st["x"], m, w_in, rope_swa if st["rope"] else ident,
                                            tqa=WINDOW if st["rope"] else st["s"], **common)
                ctx = (kc, vc, past) if st["rope"] else None
                attn = _gqa_attention(q, ka, va, sink_b, ctx, seq=st["s"], windowed=st["rope"])
                pieces = [attn]
                if not st["rope"]:
                    outs["k"].append(kf.reshape(st["b"], st["s"], SWA_KV, SWA_HD))
                    outs["v"].append(vf.reshape(st["b"], st["s"], SWA_KV, SWA_HD))
            st["x1"], h2, lg = _out_ln(pieces, w_out, st["x"], m, ln1_g[l][None], ln1_b[l][None],
                                       wr, alpha=alpha, rows_per_mod=st["tiles_per_mod"] * ROW_TILE)
            st["xs"], st["gate"], st["pt"] = _route(lg, h2, tri, seq=st["s"], cap=st["cap"], nsb=st["nsb"],
                                                    n_exp=n_exp)

        yp, ys = _experts(streams["p"]["xs"], streams["s"]["xs"], streams["p"]["gate"], streams["s"]["gate"],
                          w_gate, w_up, w_down, l)
        for st, y in ((streams["p"], yp), (streams["s"], ys)):
            st["x"] = _combine(st["pt"], y, st["x1"], mods[l, st["mrow"]], ln2_g[l][None], ln2_b[l][None],
                               alpha=alpha, seq=st["s"], cap=st["cap"],
                               rows_per_mod=st["tiles_per_mod"] * ROW_TILE)

    y_prompt = streams["p"]["x"].reshape(bp, sp, d)
    y_sample = streams["s"]["x"].reshape(bs, ss, d)
    return (y_prompt, y_sample, jnp.stack(outs["ckv"], axis=1), jnp.stack(outs["kpe"], axis=1),
            jnp.stack(outs["k"], axis=1), jnp.stack(outs["v"], axis=1))
```

```python
import functools
import math

import numpy as np
import jax
import jax.numpy as jnp
from jax import lax
from jax.experimental import pallas as pl
from jax.experimental.pallas import tpu as pltpu

f32 = jnp.float32
bf16 = jnp.bfloat16

MLA_HEADS = 8
QK_NOPE = 128
QK_ROPE = 64
V_HEAD = 128
POOL_WINDOWS = (2, 4, 8, 16)
SWA_KV = 4
SWA_HD = 64
WINDOW = 128
CAP_FACTOR = 2
GRID_W = 64
ROPE_BASE = 10000.0
RMS_EPS = 1e-6
LN_EPS = 1e-5

LOG2E = 1.4426950408889634
LANES = 128
ROW_TILE = 256
MIB = 1024 * 1024


def _cparams(sem, vmem_mib):
    return pltpu.CompilerParams(dimension_semantics=sem, vmem_limit_bytes=vmem_mib * MIB)


def _dot(a, b):
    return jnp.dot(a, b, preferred_element_type=f32)


def _dot_nt(a, b):
    return lax.dot_general(a, b, (((1,), (1,)), ((), ())), preferred_element_type=f32)


def _silu(x):
    return x / (1.0 + jnp.exp(-x))


def _rope(r, c, sa, sb):
    return r * c + pltpu.roll(r, LANES - 1, axis=1) * sa + pltpu.roll(r, 1, axis=1) * sb


def _pipelined(n, produce, consume):
    produce(0)
    for i in range(n):
        if i + 1 < n:
            produce(i + 1)
        consume(i)


def _layer_norm(z, g, b):
    mu = jnp.mean(z, axis=-1, keepdims=True)
    zc = z - mu
    var = jnp.mean(zc * zc, axis=-1, keepdims=True)
    return zc * lax.rsqrt(var + LN_EPS) * g + b


def _mod_kernel(c_ref, w_ref, b_ref, o_ref):
    a = _silu(c_ref[...]).astype(bf16)
    o_ref[0] = _dot(a, w_ref[0].astype(bf16)) + b_ref[0]


def _modulation(cond, w_ada, b_ada):
    depth, d, n = w_ada.shape
    r = cond.shape[0]
    tn = 1024
    return pl.pallas_call(
        _mod_kernel,
        out_shape=jax.ShapeDtypeStruct((depth, r, n), f32),
        grid=(depth, n // tn),
        in_specs=[pl.BlockSpec((r, d), lambda l, j: (0, 0)),
                  pl.BlockSpec((1, d, tn), lambda l, j: (l, 0, j)),
                  pl.BlockSpec((1, 1, tn), lambda l, j: (l, 0, j))],
        out_specs=pl.BlockSpec((1, r, tn), lambda l, j: (l, 0, j)),
        compiler_params=_cparams(("arbitrary", "arbitrary"), 40),
        name="adaln_modulation",
    )(cond, w_ada, b_ada.reshape(depth, 1, n))


def _in_even_kernel(x_ref, mod_ref, win_ref, qn_ref, kvn_ref, wuq_ref, wukv_ref, rc_ref, rsa_ref, rsb_ref,
                    q_ref, ckv_ref, kv_ref, kper_ref, kpea_ref, u_ref, proj_sc, *, rope, q_lora, kv_lora, qscale):
    sh = mod_ref[0, 0:1, :]
    sc = 1.0 + mod_ref[0, 1:2, :]
    o_kv = q_lora
    o_pe = q_lora + kv_lora
    o_u = o_pe + LANES
    sub = proj_sc.shape[1]

    def project(i):
        h = (x_ref[i * sub:(i + 1) * sub, :] * sc + sh).astype(bf16)
        proj_sc[i % 2] = _dot(h, win_ref[...])

    def finish(i):
        rows = slice(i * sub, (i + 1) * sub)
        pq = proj_sc[i % 2, :, 0:o_kv]
        pkv = proj_sc[i % 2, :, o_kv:o_pe]
        kpe = proj_sc[i % 2, :, o_pe:o_u]
        u_ref[rows, :] = proj_sc[i % 2, :, o_u:]
        cq = pq * lax.rsqrt(jnp.mean(pq * pq, axis=-1, keepdims=True) + RMS_EPS) * qn_ref[...]
        ckv = pkv * lax.rsqrt(jnp.mean(pkv * pkv, axis=-1, keepdims=True) + RMS_EPS) * kvn_ref[...]
        ckv_ref[rows, :] = ckv
        kv_ref[rows, :] = _dot(ckv.astype(bf16), wukv_ref[...]).astype(bf16)
        kper_ref[rows, :] = kpe
        q = _dot(cq.astype(bf16), wuq_ref[...]) * qscale
        if rope:
            c, sa, sb = rc_ref[rows, :], rsa_ref[rows, :], rsb_ref[rows, :]
            kpea_ref[rows, :] = _rope(kpe, c, sa, sb).astype(bf16)
        else:
            kpea_ref[rows, :] = kpe.astype(bf16)
        for hd in range(MLA_HEADS):
            a = 2 * hd * LANES
            q_ref[rows, a:a + LANES] = q[:, a:a + LANES].astype(bf16)
            r = q[:, a + LANES:a + 2 * LANES]
            if rope:
                r = _rope(r, c, sa, sb)
            q_ref[rows, a + LANES:a + 2 * LANES] = r.astype(bf16)

    _pipelined(x_ref.shape[0] // sub, project, finish)


def _in_even(x, mods, w_in, qn, kvn, wuq, wukv, rtab, *, rope, rows_per_mod, seq):
    t, d = x.shape
    q_lora, kv_lora = qn.shape[1], kvn.shape[1]
    n_in = w_in.shape[1]
    n_u = n_in - q_lora - kv_lora - LANES
    nq, nkv = wuq.shape[1], wukv.shape[1]
    tm = ROW_TILE * SUB_TILES
    assert t % tm == 0 and rows_per_mod % tm == 0
    tiles_per_mod = rows_per_mod // tm
    tiles_per_seq = max(1, seq // tm)
    row = lambda i: (i, 0)
    const = lambda i: (0, 0)
    rmap = lambda i: (i % tiles_per_seq, 0)
    kern = functools.partial(_in_even_kernel, rope=rope, q_lora=q_lora, kv_lora=kv_lora,
                             qscale=(QK_NOPE + QK_ROPE) ** -0.5 * LOG2E)
    return pl.pallas_call(
        kern,
        out_shape=(jax.ShapeDtypeStruct((t, nq), bf16), jax.ShapeDtypeStruct((t, kv_lora), f32),
                   jax.ShapeDtypeStruct((t, nkv), bf16), jax.ShapeDtypeStruct((t, LANES), f32),
                   jax.ShapeDtypeStruct((t, LANES), bf16), jax.ShapeDtypeStruct((t, n_u), f32)),
        grid=(t // tm,),
        in_specs=[pl.BlockSpec((tm, d), row),
                  pl.BlockSpec((1, 6, d), lambda i: (i // tiles_per_mod, 0, 0)),
                  pl.BlockSpec((d, n_in), const),
                  pl.BlockSpec((1, q_lora), const), pl.BlockSpec((1, kv_lora), const),
                  pl.BlockSpec((q_lora, nq), const), pl.BlockSpec((kv_lora, nkv), const),
                  pl.BlockSpec((tm, LANES), rmap), pl.BlockSpec((tm, LANES), rmap), pl.BlockSpec((tm, LANES), rmap)],
        out_specs=(pl.BlockSpec((tm, nq), row), pl.BlockSpec((tm, kv_lora), row), pl.BlockSpec((tm, nkv), row),
                   pl.BlockSpec((tm, LANES), row), pl.BlockSpec((tm, LANES), row), pl.BlockSpec((tm, n_u), row)),
        scratch_shapes=[pltpu.VMEM((2, ROW_TILE, n_in), f32)],
        compiler_params=_cparams(("arbitrary",), 56),
        name="in_proj_even",
    )(x, mods, w_in, qn, kvn, wuq, wukv, *rtab)


def _mm_kernel(x_ref, w_ref, o_ref):
    o_ref[...] = _dot(x_ref[...].astype(bf16), w_ref[...]).astype(o_ref.dtype)


def _matmul(x, w, out_dtype, tm=512):
    m, k = x.shape
    n = w.shape[1]
    tm = min(tm, m)
    assert m % tm == 0
    return pl.pallas_call(
        _mm_kernel,
        out_shape=jax.ShapeDtypeStruct((m, n), out_dtype),
        grid=(m // tm,),
        in_specs=[pl.BlockSpec((tm, k), lambda i: (i, 0)), pl.BlockSpec((k, n), lambda i: (0, 0))],
        out_specs=pl.BlockSpec((tm, n), lambda i: (i, 0)),
        compiler_params=_cparams(("arbitrary",), 24),
        name="ctx_kv_up",
    )(x, w)


def _pool_kernel(u_ref, wp_ref, ps_ref, o_ref, *, seq):
    gc = wp_ref.shape[1]
    t = lax.broadcasted_iota(jnp.int32, (seq, 1), 0)

    def down(a, k):
        return jnp.where(t >= k, pltpu.roll(a, k, axis=0), 0.0)

    def up(a, k):
        return jnp.where(t < seq - k, pltpu.roll(a, seq - k, axis=0), 0.0)

    for gi, w in enumerate(POOL_WINDOWS):
        half = w // 2
        a = u_ref[:, gi * gc:(gi + 1) * gc]
        trail, lead, k = a, a, 1
        while k < half:
            trail = trail + down(trail, k)
            lead = lead + up(lead, k)
            k *= 2
        win = down(trail, 1) + lead
        cnt = (jnp.minimum(t + half, seq) - jnp.maximum(t - half, 0)).astype(f32)
        pooled = win / cnt - a
        mixed = _dot(pooled.astype(bf16), wp_ref[gi])
        o_ref[:, gi * gc:(gi + 1) * gc] = (mixed * ps_ref[:, gi * gc:(gi + 1) * gc]).astype(bf16)


def _pool(u, w_pool, pool_scale, seq):
    t, ch = u.shape
    g, gc, _ = w_pool.shape
    return pl.pallas_call(
        functools.partial(_pool_kernel, seq=seq),
        out_shape=jax.ShapeDtypeStruct((t, ch), bf16),
        grid=(t // seq,),
        in_specs=[pl.BlockSpec((seq, ch), lambda b: (b, 0)),
                  pl.BlockSpec((g, gc, gc), lambda b: (0, 0, 0)),
                  pl.BlockSpec((1, ch), lambda b: (0, 0))],
        out_specs=pl.BlockSpec((seq, ch), lambda b: (b, 0)),
        compiler_params=_cparams(("arbitrary",), 40),
        name="pool_mixer",
    )(u, w_pool, pool_scale)


def _mla_kernel(*refs, has_ctx):
    if has_ctx:
        q_ref, kv_ref, kpe_ref, kvc_ref, kpec_ref, o_ref, sc_sc = refs
    else:
        q_ref, kv_ref, kpe_ref, o_ref, sc_sc = refs
    seq = kv_ref.shape[0]

    def score(hd):
        a = 2 * hd * LANES
        q = q_ref[:, a:a + 2 * LANES]
        k = jnp.concatenate([kv_ref[:, a:a + LANES], kpe_ref[...]], axis=1)
        sc_sc[hd % 2, :, 0:seq] = _dot_nt(q, k)
        if has_ctx:
            kc = jnp.concatenate([kvc_ref[:, a:a + LANES], kpec_ref[...]], axis=1)
            sc_sc[hd % 2, :, seq:] = _dot_nt(q, kc)

    def attend(hd):
        a = 2 * hd * LANES
        x = sc_sc[hd % 2]
        m = x.max(axis=-1, keepdims=True)
        p = jnp.exp2(x - m)
        l = p.sum(axis=-1, keepdims=True)
        p = p.astype(bf16)
        o = _dot(p[:, 0:seq], kv_ref[:, a + LANES:a + 2 * LANES])
        if has_ctx:
            o = o + _dot(p[:, seq:], kvc_ref[:, a + LANES:a + 2 * LANES])
        o_ref[:, hd * LANES:(hd + 1) * LANES] = (o / l).astype(bf16)

    score(0)
    for hd in range(MLA_HEADS):
        if hd + 1 < MLA_HEADS:
            score(hd + 1)
        attend(hd)


def _mla_attention(q, kv, kpe, ctx, seq):
    t, nq = q.shape
    tq = ROW_TILE
    tps = seq // tq
    in_specs = [pl.BlockSpec((tq, nq), lambda b, j: (b * tps + j, 0)),
                pl.BlockSpec((seq, nq), lambda b, j: (b, 0)),
                pl.BlockSpec((seq, LANES), lambda b, j: (b, 0))]
    args = [q, kv, kpe]
    if ctx is not None:
        kvc, kpec, past = ctx
        in_specs += [pl.BlockSpec((past, nq), lambda b, j: (b, 0)),
                     pl.BlockSpec((past, LANES), lambda b, j: (b, 0))]
        args += [kvc, kpec]
    n_keys = seq + (ctx[2] if ctx is not None else 0)
    return pl.pallas_call(
        functools.partial(_mla_kernel, has_ctx=ctx is not None),
        out_shape=jax.ShapeDtypeStruct((t, MLA_HEADS * V_HEAD), bf16),
        grid=(t // seq, tps),
        in_specs=in_specs,
        out_specs=pl.BlockSpec((tq, MLA_HEADS * V_HEAD), lambda b, j: (b * tps + j, 0)),
        scratch_shapes=[pltpu.VMEM((2, tq, n_keys), f32)],
        compiler_params=_cparams(("arbitrary", "arbitrary"), 48),
        name="mla_attention",
    )(*args)


SUB_TILES = 2
SUB_ROWS = 128


def _out_ln_kernel(*refs, n_pieces, alpha):
    pieces = refs[:n_pieces]
    w_ref, x_ref, mod_ref, g_ref, b_ref, wr_ref, x1_ref, h2_ref, lg_ref, mix_sc = refs[n_pieces:]
    g1 = mod_ref[0, 2:3, :]
    sc2 = 1.0 + mod_ref[0, 4:5, :]
    sh2 = mod_ref[0, 3:4, :]
    sub = mix_sc.shape[1]

    def project(i):
        rows = slice(i * sub, (i + 1) * sub)
        off = 0
        mix = None
        for p in pieces:
            kp = p.shape[1]
            part = _dot(p[rows, :], w_ref[off:off + kp, :])
            mix = part if mix is None else mix + part
            off += kp
        mix_sc[i % 2] = mix

    def finish(i):
        rows = slice(i * sub, (i + 1) * sub)
        x1 = _layer_norm(alpha * x_ref[rows, :] + g1 * mix_sc[i % 2], g_ref[...], b_ref[...])
        x1_ref[rows, :] = x1
        h2 = x1 * sc2 + sh2
        h2h = h2.astype(bf16)
        h2_ref[rows, :] = h2h
        h2l = (h2 - h2h.astype(f32)).astype(bf16)
        hh = _dot(h2h, wr_ref[...])
        lg_ref[rows, :] = hh[:, 0:LANES] + hh[:, LANES:] + _dot(h2l, wr_ref[:, 0:LANES])

    _pipelined(x_ref.shape[0] // sub, project, finish)


def _out_ln(pieces, w_out, x, mods, ln_g, ln_b, wr, *, alpha, rows_per_mod):
    t, d = x.shape
    tm = ROW_TILE * SUB_TILES
    assert t % tm == 0 and rows_per_mod % tm == 0
    row = lambda i: (i, 0)
    const = lambda i: (0, 0)
    in_specs = [pl.BlockSpec((tm, p.shape[1]), row) for p in pieces]
    in_specs += [pl.BlockSpec(w_out.shape, const), pl.BlockSpec((tm, d), row),
                 pl.BlockSpec((1, 6, d), lambda i: (i // (rows_per_mod // tm), 0, 0)),
                 pl.BlockSpec((1, d), const), pl.BlockSpec((1, d), const),
                 pl.BlockSpec((d, 2 * LANES), const)]
    return pl.pallas_call(
        functools.partial(_out_ln_kernel, n_pieces=len(pieces), alpha=alpha),
        out_shape=(jax.ShapeDtypeStruct((t, d), f32), jax.ShapeDtypeStruct((t, d), bf16),
                   jax.ShapeDtypeStruct((t, LANES), f32)),
        grid=(t // tm,),
        in_specs=in_specs,
        out_specs=(pl.BlockSpec((tm, d), row), pl.BlockSpec((tm, d), row), pl.BlockSpec((tm, LANES), row)),
        scratch_shapes=[pltpu.VMEM((2, SUB_ROWS, d), f32)],
        compiler_params=_cparams(("arbitrary",), 56),
        name="out_proj_ln",
    )(*pieces, w_out, x, mods, ln_g, ln_b, wr)


SEARCH_BITS = 4
ROUTE_TOKENS = 1024
DISPATCH_ROWS = 512


def _kth_largest_bits(bits, cap):
    thr = jnp.zeros((bits.shape[0], 1), jnp.int32)
    hi = 31
    while hi > 0:
        lo = max(hi - SEARCH_BITS, 0) if hi % SEARCH_BITS == 0 else hi - hi % SEARCH_BITS
        digit = jnp.zeros_like(thr)
        for c in range(1, 1 << (hi - lo)):
            cnt = jnp.sum(jnp.where(bits >= (thr | (c << lo)), 1.0, 0.0), axis=1, keepdims=True)
            digit = digit + jnp.where(cnt >= cap, 1, 0)
        thr = thr | (digit * (1 << lo))
        hi = lo
    return thr


def _lane_prefix(mask, tri):
    n_rows, n = mask.shape
    blocks = [mask[:, k:k + LANES] for k in range(0, n, LANES)]
    local = _dot(jnp.concatenate(blocks, axis=0).astype(bf16), tri)
    out, carry = [], None
    for i in range(len(blocks)):
        blk = local[i * n_rows:(i + 1) * n_rows, :]
        out.append(blk if carry is None else blk + carry)
        total = blk[:, LANES - 1:LANES]
        carry = total if carry is None else carry + total
    return jnp.concatenate(out, axis=1)


def _route_kernel(lg_ref, h2_ref, tri_ref, xs_ref, gate_ref, pt_ref, aff_sc, pos_sc, *, cap, seq, n_exp, nsb):
    tri = tri_ref[...]
    pos_sc[...] = jnp.full(pos_sc.shape, -1.0, f32)
    for sb in range(nsb):
        lg = lg_ref[sb * seq:(sb + 1) * seq, :].T[0:n_exp, :]
        z = jnp.exp(lg - lg.max(axis=0, keepdims=True))
        aff = z / z.sum(axis=0, keepdims=True)
        bits = pltpu.bitcast(aff, jnp.int32)
        thr = _kth_largest_bits(bits, cap)
        gt = bits > thr
        eq = bits == thr
        need = cap - jnp.sum(jnp.where(gt, 1.0, 0.0), axis=1, keepdims=True)
        eq_rank = _lane_prefix(jnp.where(eq, 1.0, 0.0), tri)
        sel = gt | (eq & (eq_rank <= need))
        sel_rank = _lane_prefix(jnp.where(sel, 1.0, 0.0), tri)
        aff_sc[sb] = aff
        pos_sc[sb, 0:n_exp, :] = jnp.where(sel, sel_rank - 1.0, -1.0)

    slot_col = lax.broadcasted_iota(jnp.int32, (cap, seq), 0).astype(f32)

    eg = max(1, min(n_exp, DISPATCH_ROWS // cap))

    def dispatch(g, carry):
        for sb in range(nsb):
            rows = slice(sb * cap, (sb + 1) * cap)
            onehot = []
            for k in range(eg):
                e = g * eg + k
                hit = pos_sc[sb, pl.ds(e, 1), :] == slot_col
                onehot.append(jnp.where(hit, 1.0, 0.0).astype(bf16))
                gate = jnp.sum(jnp.where(hit, aff_sc[sb, pl.ds(e, 1), :], 0.0), axis=1, keepdims=True)
                gate_ref[e, rows, :] = jnp.broadcast_to(gate, (cap, LANES))
            picked = _dot(jnp.concatenate(onehot, axis=0), h2_ref[sb * seq:(sb + 1) * seq, :])
            for k in range(eg):
                xs_ref[g * eg + k, rows, :] = picked[k * cap:(k + 1) * cap, :].astype(bf16)
        return carry

    lax.fori_loop(0, n_exp // eg, dispatch, 0)

    lane = lax.broadcasted_iota(jnp.int32, (1, LANES), 1)
    for sb in range(nsb):
        pos_t = pos_sc[sb].T
        for j in range(n_exp * cap // LANES):
            if cap >= LANES:
                e = j * LANES // cap
                target = (lane + (j * LANES - e * cap)).astype(f32)
                hit = pos_t[:, e:e + 1] == target
            else:
                per = LANES // cap
                hit = None
                for k in range(per):
                    e = j * per + k
                    target = jnp.where((lane >= k * cap) & (lane < (k + 1) * cap), lane - k * cap, -7).astype(f32)
                    m = pos_t[:, e:e + 1] == target
                    hit = m if hit is None else hit | m
            pt_ref[sb * seq:(sb + 1) * seq, j * LANES:(j + 1) * LANES] = jnp.where(hit, 1.0, 0.0).astype(bf16)


def _route(lg, h2, tri, *, seq, cap, nsb, n_exp):
    t, d = h2.shape
    nb = t // seq
    assert nb % nsb == 0
    return pl.pallas_call(
        functools.partial(_route_kernel, cap=cap, seq=seq, n_exp=n_exp, nsb=nsb),
        out_shape=(jax.ShapeDtypeStruct((n_exp, nb * cap, d), bf16),
                   jax.ShapeDtypeStruct((n_exp, nb * cap, LANES), f32),
                   jax.ShapeDtypeStruct((t, n_exp * cap), bf16)),
        grid=(nb // nsb,),
        in_specs=[pl.BlockSpec((nsb * seq, LANES), lambda b: (b, 0)),
                  pl.BlockSpec((nsb * seq, d), lambda b: (b, 0)),
                  pl.BlockSpec((LANES, LANES), lambda b: (0, 0))],
        out_specs=(pl.BlockSpec((n_exp, nsb * cap, d), lambda b: (0, b, 0)),
                   pl.BlockSpec((n_exp, nsb * cap, LANES), lambda b: (0, b, 0)),
                   pl.BlockSpec((nsb * seq, n_exp * cap), lambda b: (b, 0))),
        scratch_shapes=[pltpu.VMEM((nsb, n_exp, seq), f32), pltpu.VMEM((nsb, LANES, seq), f32)],
        compiler_params=_cparams(("arbitrary",), 56),
        name="route_dispatch",
    )(lg, h2, tri)


EXPERT_TILE = 512
EXPERT_ROWS = 256


def _expert_kernel(xp_ref, xs_ref, gp_ref, gs_ref, wg_ref, wu_ref, wd_ref, yp_ref, ys_ref,
                   hid_sc, wa_sc, wb_sc, wc_sc, *, nf, tf):
    j = pl.program_id(1)
    mp, ms = xp_ref.shape[1], xs_ref.shape[1]
    cp, cs = min(EXPERT_ROWS, mp), min(EXPERT_ROWS, ms)
    chunks = [(xp_ref, gp_ref, yp_ref, r, r, cp) for r in range(0, mp, cp)]
    chunks += [(xs_ref, gs_ref, ys_ref, r, mp + r, cs) for r in range(0, ms, cs)]

    @pl.when(j < nf)
    def _():
        wa_sc[...] = wg_ref[0, 0].astype(bf16)
        wb_sc[...] = wu_ref[0, 0].astype(bf16)
        for x_ref, _, _, r, m, n in chunks:
            xc = x_ref[0, r:r + n, :]
            g = _dot(xc, wa_sc[...])
            u = _dot(xc, wb_sc[...])
            hid_sc[j, m:m + n, :] = (_silu(g) * u).astype(bf16)

    @pl.when(j >= nf)
    def _():
        wc_sc[...] = wd_ref[0, 0].astype(bf16)
        for _, g_ref, y_ref, r, m, n in chunks:
            acc = _dot(hid_sc[0, m:m + n, :], wc_sc[0:tf, :])
            for f in range(1, nf):
                acc = acc + _dot(hid_sc[f, m:m + n, :], wc_sc[f * tf:(f + 1) * tf, :])
            y_ref[0, r:r + n, :] = (acc * g_ref[0, r:r + n, 0:1]).astype(bf16)


def _experts(xp, xs, gp, gs, w_gate, w_up, w_down, layer):
    n_exp, mp, d = xp.shape
    ms = xs.shape[1]
    ff = w_gate.shape[3]
    tf = tn = EXPERT_TILE
    nf, nn = ff // tf, d // tn
    x_map = lambda e, j: (e, 0, 0)
    up_map = lambda e, j: (layer, e, 0, jnp.minimum(j, nf - 1))
    dn_map = lambda e, j: (layer, e, 0, jnp.maximum(j - nf, 0))
    y_map = lambda e, j: (e, 0, jnp.maximum(j - nf, 0))
    return pl.pallas_call(
        functools.partial(_expert_kernel, nf=nf, tf=tf),
        out_shape=(jax.ShapeDtypeStruct((n_exp, mp, d), bf16), jax.ShapeDtypeStruct((n_exp, ms, d), bf16)),
        grid=(n_exp, nf + nn),
        in_specs=[pl.BlockSpec((1, mp, d), x_map), pl.BlockSpec((1, ms, d), x_map),
                  pl.BlockSpec((1, mp, LANES), x_map), pl.BlockSpec((1, ms, LANES), x_map),
                  pl.BlockSpec((1, 1, d, tf), up_map), pl.BlockSpec((1, 1, d, tf), up_map),
                  pl.BlockSpec((1, 1, ff, tn), dn_map)],
        out_specs=(pl.BlockSpec((1, mp, tn), y_map), pl.BlockSpec((1, ms, tn), y_map)),
        scratch_shapes=[pltpu.VMEM((nf, mp + ms, tf), bf16), pltpu.VMEM((d, tf), bf16),
                        pltpu.VMEM((d, tf), bf16), pltpu.VMEM((ff, tn), bf16)],
        compiler_params=_cparams(("arbitrary", "arbitrary"), 58),
        name="expert_swiglu",
    )(xp, xs, gp, gs, w_gate, w_up, w_down)


def _combine_kernel(pt_ref, y_ref, x_ref, mod_ref, g_ref, b_ref, o_ref, ffn_sc, *, alpha, seq, cap):
    n_exp, _, d = y_ref.shape
    g2 = mod_ref[0, 5:6, :]
    sub = ffn_sc.shape[1]

    def scatter(i):
        sq = i * sub // seq
        y = y_ref[:, sq * cap:(sq + 1) * cap, :].reshape(n_exp * cap, d)
        ffn_sc[i % 2] = _dot(pt_ref[i * sub:(i + 1) * sub, :], y)

    def finish(i):
        rows = slice(i * sub, (i + 1) * sub)
        o_ref[rows, :] = _layer_norm(alpha * x_ref[rows, :] + g2 * ffn_sc[i % 2], g_ref[...], b_ref[...])

    _pipelined(x_ref.shape[0] // sub, scatter, finish)


def _combine(pt, y, x1, mods, ln_g, ln_b, *, alpha, seq, cap, rows_per_mod):
    t, d = x1.shape
    n_exp = y.shape[0]
    tm = ROW_TILE * SUB_TILES
    assert t % tm == 0 and rows_per_mod % tm == 0 and (tm % seq == 0 or seq % tm == 0)
    spt = max(1, tm // seq)
    tps = max(1, seq // tm)
    row = lambda i: (i, 0)
    const = lambda i: (0, 0)
    return pl.pallas_call(
        functools.partial(_combine_kernel, alpha=alpha, seq=seq, cap=cap),
        out_shape=jax.ShapeDtypeStruct((t, d), f32),
        grid=(t // tm,),
        in_specs=[pl.BlockSpec((tm, n_exp * cap), row),
                  pl.BlockSpec((n_exp, spt * cap, d), lambda i: (0, i // tps, 0)),
                  pl.BlockSpec((tm, d), row),
                  pl.BlockSpec((1, 6, d), lambda i: (i // (rows_per_mod // tm), 0, 0)),
                  pl.BlockSpec((1, d), const), pl.BlockSpec((1, d), const)],
        out_specs=pl.BlockSpec((tm, d), row),
        scratch_shapes=[pltpu.VMEM((2, ROW_TILE, d), f32)],
        compiler_params=_cparams(("arbitrary",), 56),
        name="combine_ln",
    )(pt, y, x1, mods, ln_g, ln_b)


def _pad_pairs(chunk, lane):
    swapped = pltpu.roll(chunk, LANES // 2, axis=1)
    lo = lane < LANES // 2
    return [jnp.where(lo, chunk, 0.0), jnp.where(lo, 0.0, swapped),
            jnp.where(lo, swapped, 0.0), jnp.where(lo, 0.0, chunk)]


def _in_odd_kernel(x_ref, mod_ref, win_ref, rc_ref, rsa_ref, rsb_ref,
                   q_ref, ka_ref, va_ref, kf_ref, vf_ref, proj_sc, *, rope, nq, nkv, qscale, tqa):
    sh = mod_ref[0, 0:1, :]
    sc = 1.0 + mod_ref[0, 1:2, :]
    pairs = nq // LANES // SWA_KV
    lane = lax.broadcasted_iota(jnp.int32, (1, LANES), 1)
    sub = proj_sc.shape[1]
    piece = min(sub, tqa)

    def project(i):
        h = (x_ref[i * sub:(i + 1) * sub, :] * sc + sh).astype(bf16)
        proj_sc[i % 2] = _dot(h, win_ref[...])

    def finish(i):
        rows = slice(i * sub, (i + 1) * sub)
        k = proj_sc[i % 2, :, nq:nq + nkv]
        v = proj_sc[i % 2, :, nq + nkv:nq + 2 * nkv]
        kf_ref[rows, :] = k
        vf_ref[rows, :] = v
        if rope:
            c, sa, sb = rc_ref[rows, :], rsa_ref[rows, :], rsb_ref[rows, :]
        for ci in range(nq // LANES):
            r = proj_sc[i % 2, :, ci * LANES:(ci + 1) * LANES]
            if rope:
                r = _rope(r, c, sa, sb)
            r = (r * qscale).astype(bf16)
            hk, pr = ci // pairs, ci % pairs
            for p0 in range(0, sub, piece):
                tok = i * sub + p0
                q_ref[tok // tqa, hk, pr * tqa + tok % tqa:pr * tqa + tok % tqa + piece, :] = r[p0:p0 + piece, :]
        for a in range(0, nkv, LANES):
            kc = k[:, a:a + LANES]
            if rope:
                kc = _rope(kc, c, sa, sb)
            for j, (kp, vp) in enumerate(zip(_pad_pairs(kc, lane), _pad_pairs(v[:, a:a + LANES], lane))):
                o = 4 * a + j * LANES
                ka_ref[rows, o:o + LANES] = kp.astype(bf16)
                va_ref[rows, o:o + LANES] = vp.astype(bf16)

    _pipelined(x_ref.shape[0] // sub, project, finish)


def _in_odd(x, mods, w_in, rtab, *, rope, tqa, rows_per_mod, seq):
    t, d = x.shape
    n_in = w_in.shape[1]
    nkv = SWA_KV * SWA_HD
    nq = n_in - 2 * nkv
    tm = ROW_TILE * SUB_TILES
    assert tm % tqa == 0 and ROW_TILE % tqa == 0 and t % tm == 0 and rows_per_mod % tm == 0
    tiles_per_mod = rows_per_mod // tm
    tiles_per_seq = max(1, seq // tm)
    q_rows = nq // LANES // SWA_KV * tqa
    row = lambda i: (i, 0)
    rmap = lambda i: (i % tiles_per_seq, 0)
    kern = functools.partial(_in_odd_kernel, rope=rope, nq=nq, nkv=nkv, qscale=SWA_HD ** -0.5 * LOG2E, tqa=tqa)
    return pl.pallas_call(
        kern,
        out_shape=(jax.ShapeDtypeStruct((t // tqa, SWA_KV, q_rows, LANES), bf16),
                   jax.ShapeDtypeStruct((t, 4 * nkv), bf16),
                   jax.ShapeDtypeStruct((t, 4 * nkv), bf16), jax.ShapeDtypeStruct((t, nkv), f32),
                   jax.ShapeDtypeStruct((t, nkv), f32)),
        grid=(t // tm,),
        in_specs=[pl.BlockSpec((tm, d), row),
                  pl.BlockSpec((1, 6, d), lambda i: (i // tiles_per_mod, 0, 0)),
                  pl.BlockSpec((d, n_in), lambda i: (0, 0)),
                  pl.BlockSpec((tm, LANES), rmap), pl.BlockSpec((tm, LANES), rmap), pl.BlockSpec((tm, LANES), rmap)],
        out_specs=(pl.BlockSpec((tm // tqa, SWA_KV, q_rows, LANES), lambda i: (i, 0, 0, 0)),
                   pl.BlockSpec((tm, 4 * nkv), row), pl.BlockSpec((tm, 4 * nkv), row),
                   pl.BlockSpec((tm, nkv), row), pl.BlockSpec((tm, nkv), row)),
        scratch_shapes=[pltpu.VMEM((2, ROW_TILE, n_in), f32)],
        compiler_params=_cparams(("arbitrary",), 56),
        name="in_proj_odd",
    )(x, mods, w_in, *rtab)


def _gqa_kernel(*refs, windowed, seq, tq):
    if windowed:
        q_ref, ka_ref, va_ref, kc_ref, vc_ref, sink_ref, o_ref, sc_sc = refs
    else:
        q_ref, ka_ref, va_ref, sink_ref, o_ref, sc_sc = refs
    pairs = q_ref.shape[2] // tq
    if windowed:
        span = tq + 2 * WINDOW
        q0 = pl.program_id(1) * tq
        start = pl.multiple_of(jnp.clip(q0 - WINDOW, 0, seq - span), LANES)
        kpos = start + lax.broadcasted_iota(jnp.int32, (tq, span), 1)
        qpos = q0 + lax.broadcasted_iota(jnp.int32, (tq, span), 0)
        bias1 = jnp.where(jnp.abs(kpos - qpos) <= WINDOW, 0.0, -1e30)
        bias = jnp.concatenate([bias1] * pairs, axis=0)
        rows = pl.ds(start, span)
    else:
        rows = slice(None)
    n_loc = span if windowed else seq

    def chunk(u):
        return slice(u * LANES, (u + 1) * LANES)

    def score(u):
        q = q_ref[0, u // 2]
        sc = _dot_nt(q, ka_ref[rows, chunk(u)])
        if windowed:
            sc_sc[u % 2, :, 0:n_loc] = sc + bias
            sc_sc[u % 2, :, n_loc:] = _dot_nt(q, kc_ref[:, chunk(u)])
        else:
            sc_sc[u % 2] = sc

    def attend(u):
        heads = [2 * (u // 2 * pairs + pr) + u % 2 for pr in range(pairs)]
        sink = jnp.concatenate([jnp.broadcast_to(sink_ref[h:h + 1, 0:1], (tq, 1)) for h in heads], axis=0) * LOG2E
        x = sc_sc[u % 2]
        m = jnp.maximum(sink, x.max(axis=-1, keepdims=True))
        p = jnp.exp2(x - m)
        l = jnp.exp2(sink - m) + p.sum(axis=-1, keepdims=True)
        p = p.astype(bf16)
        o = _dot(p[:, 0:n_loc], va_ref[rows, chunk(u)])
        if windowed:
            o = o + _dot(p[:, n_loc:], vc_ref[:, chunk(u)])
        return o / l

    n_units = 2 * SWA_KV
    score(0)
    half = None
    for u in range(n_units):
        if u + 1 < n_units:
            score(u + 1)
        o = attend(u)
        if u % 2 == 0:
            half = o
        else:
            out = half + o
            for pr in range(pairs):
                col = (u // 2 * pairs + pr) * LANES
                o_ref[:, col:col + LANES] = out[pr * tq:(pr + 1) * tq, :].astype(bf16)


def _gqa_attention(q, ka, va, sink, ctx, *, seq, windowed):
    t, nk = ka.shape
    tq = WINDOW if windowed else seq
    tps = seq // tq
    q_rows = q.shape[2]
    nq = q_rows // tq * SWA_KV * LANES
    in_specs = [pl.BlockSpec((1, SWA_KV, q_rows, LANES), lambda b, j: (b * tps + j, 0, 0, 0)),
                pl.BlockSpec((seq, nk), lambda b, j: (b, 0)),
                pl.BlockSpec((seq, nk), lambda b, j: (b, 0))]
    args = [q, ka, va]
    if windowed:
        kc, vc, past = ctx
        in_specs += [pl.BlockSpec((past, nk), lambda b, j: (b, 0)), pl.BlockSpec((past, nk), lambda b, j: (b, 0))]
        args += [kc, vc]
    in_specs.append(pl.BlockSpec(sink.shape, lambda b, j: (0, 0)))
    args.append(sink)
    n_keys = tq + 2 * WINDOW + ctx[2] if windowed else seq
    return pl.pallas_call(
        functools.partial(_gqa_kernel, windowed=windowed, seq=seq, tq=tq),
        out_shape=jax.ShapeDtypeStruct((t, nq), bf16),
        grid=(t // seq, tps),
        in_specs=in_specs,
        out_specs=pl.BlockSpec((tq, nq), lambda b, j: (b * tps + j, 0)),
        scratch_shapes=[pltpu.VMEM((2, q_rows, n_keys), f32)],
        compiler_params=_cparams(("arbitrary", "arbitrary"), 40),
        name="gqa_attention",
    )(*args)


def _rope_tables(n_tokens, rot_dim):
    rows = n_tokens // GRID_W
    row = jnp.repeat(jnp.arange(rows, dtype=f32), GRID_W)
    col = jnp.tile(jnp.arange(GRID_W, dtype=f32), rows)
    n_freq = rot_dim // 4
    inv = ROPE_BASE ** (-jnp.arange(n_freq, dtype=f32) / n_freq)
    ang = jnp.concatenate([row[:, None] * inv, col[:, None] * inv], -1)
    cos = jnp.repeat(jnp.cos(ang), 2, axis=1)
    sin = jnp.repeat(jnp.sin(ang), 2, axis=1)
    even = (jnp.arange(rot_dim) % 2 == 0)[None, :]
    sa = jnp.where(even, -sin, 0.0)
    sb = jnp.where(even, 0.0, sin)
    reps = LANES // rot_dim
    if reps >= 1:
        return tuple(jnp.tile(a, (1, reps)) for a in (cos, sa, sb))
    raise NotImplementedError


def _pad_kv_pairs(a):
    rows = a.shape[0]
    a = a.reshape(rows, SWA_KV, SWA_HD)
    z = jnp.zeros_like(a)
    return jnp.concatenate([a, z, z, a], axis=-1).reshape(rows, SWA_KV * 4 * SWA_HD)


def _split_hi_lo(w):
    hi = w.astype(bf16)
    return hi, (w - hi.astype(f32)).astype(bf16)


def kernel(x_prompt, x_sample, cache_mla_ckv, cache_mla_kpe, cache_swa_k, cache_swa_v, c, c_ctx, w_ada, b_ada, w_in_even, q_norm, kv_norm, w_uq, w_ukv, w_pool, pool_scale, w_out_even, w_in_odd, sink, w_out_odd, ln1_g, ln1_b, w_router, w_gate, w_up, w_down, ln2_g, ln2_b):
    depth, d = w_ada.shape[0], w_ada.shape[1]
    bp, sp, _ = x_prompt.shape
    bs, ss, _ = x_sample.shape
    past = cache_mla_ckv.shape[2]
    n_exp = w_router.shape[2]
    alpha = (2 * depth) ** 0.25
    q_lora, kv_lora = q_norm.shape[1], kv_norm.shape[1]
    pool_ch = pool_scale.shape[1]
    assert sp % ROW_TILE == 0 and ss % ROW_TILE == 0 and ss % GRID_W == 0
    assert QK_ROPE <= LANES and LANES % SWA_HD == 0 and 2 * SWA_HD == LANES

    n_rows = -(-(1 + bs) // 16) * 16
    cond = jnp.zeros((n_rows, d), f32).at[0].set(c_ctx).at[1:1 + bs].set(c)
    mods = _modulation(cond, w_ada, b_ada)
    mods = mods.reshape(depth, n_rows, 6, d)

    streams = {
        "p": dict(x=x_prompt.reshape(bp * sp, d), b=bp, s=sp, rope=False, mrow=slice(0, 1),
                  tiles_per_mod=bp * sp // ROW_TILE),
        "s": dict(x=x_sample.reshape(bs * ss, d), b=bs, s=ss, rope=True, mrow=slice(1, 1 + bs),
                  tiles_per_mod=ss // ROW_TILE),
    }
    for st in streams.values():
        st["cap"] = max(1, CAP_FACTOR * st["s"] // n_exp)
        st["nsb"] = math.gcd(st["b"], max(1, ROUTE_TOKENS // st["s"]))
        assert (n_exp * st["cap"]) % LANES == 0 and (LANES % st["cap"] == 0 or st["cap"] % LANES == 0)
        m_rows = st["b"] * st["cap"]
        assert m_rows % min(EXPERT_ROWS, m_rows) == 0 and m_rows % 16 == 0

    tri = jnp.triu(jnp.ones((LANES, LANES), bf16))
    rope_mla = _rope_tables(ss, QK_ROPE)
    rope_swa = _rope_tables(ss, SWA_HD)
    ident = (jnp.ones((ROW_TILE * SUB_TILES, LANES), f32),) + (jnp.zeros((ROW_TILE * SUB_TILES, LANES), f32),) * 2

    outs = {"ckv": [], "kpe": [], "k": [], "v": []}
    for l in range(depth):
        wr = jnp.concatenate(_split_hi_lo(jnp.pad(w_router[l], ((0, 0), (0, LANES - n_exp)))), axis=1)
        if l % 2 == 0:
            e = l // 2
            wi = w_in_even[e]
            o_pe = q_lora + kv_lora
            w_in = jnp.concatenate([wi[:, :o_pe + QK_ROPE], jnp.zeros((d, LANES - QK_ROPE), f32),
                                    wi[:, o_pe + QK_ROPE:]], axis=1).astype(bf16)
            wq = w_uq[e].reshape(q_lora, MLA_HEADS, QK_NOPE + QK_ROPE)
            wq = jnp.pad(wq, ((0, 0), (0, 0), (0, 2 * LANES - QK_NOPE - QK_ROPE)))
            wq = wq.reshape(q_lora, MLA_HEADS * 2 * LANES).astype(bf16)
            wkv = w_ukv[e].astype(bf16)
            w_out = w_out_even[e].astype(bf16)
            wp = w_pool[e].astype(bf16)
            kvc = _matmul(cache_mla_ckv[:, e].reshape(bs * past, kv_lora), wkv, bf16)
            kpec = jnp.pad(cache_mla_kpe[:, e].reshape(bs * past, QK_ROPE),
                           ((0, 0), (0, LANES - QK_ROPE))).astype(bf16)
        else:
            o = l // 2
            w_in = w_in_odd[o].astype(bf16)
            w_out = w_out_odd[o].astype(bf16)
            sink_b = jnp.broadcast_to(sink[o][:, None], (sink.shape[1], LANES))
            kc = _pad_kv_pairs(cache_swa_k[:, o].reshape(bs * past, SWA_KV * SWA_HD)).astype(bf16)
            vc = _pad_kv_pairs(cache_swa_v[:, o].reshape(bs * past, SWA_KV * SWA_HD)).astype(bf16)

        for name, st in streams.items():
            m = mods[l, st["mrow"]]
            common = dict(rope=st["rope"], rows_per_mod=st["tiles_per_mod"] * ROW_TILE, seq=st["s"])
            if l % 2 == 0:
                q, ckv, kv, kpe_raw, kpe_att, u = _in_even(
                    st["x"], m, w_in, q_norm[e][None], kv_norm[e][None], wq, wkv,
                    rope_mla if st["rope"] else ident, **common)
                pooled = _pool(u, wp, pool_scale[e][None], st["s"])
                ctx = (kvc, kpec, past) if st["rope"] else None
                attn = _mla_attention(q, kv, kpe_att, ctx, st["s"])
                pieces = [attn, pooled]
                if not st["rope"]:
                    outs["ckv"].append(ckv.reshape(st["b"], st["s"], kv_lora))
                    outs["kpe"].append(kpe_raw[:, :QK_ROPE].reshape(st["b"], st["s"], QK_ROPE))
            else:
                q, ka, va, kf, vf = _in_odd(st["x"], m, w_in, rope_swa if st["rope"] else ident,
                                            tqa=WINDOW if st["rope"] else st["s"], **common)
                ctx = (kc, vc, past) if st["rope"] else None
                attn = _gqa_attention(q, ka, va, sink_b, ctx, seq=st["s"], windowed=st["rope"])
                pieces = [attn]
                if not st["rope"]:
                    outs["k"].append(kf.reshape(st["b"], st["s"], SWA_KV, SWA_HD))
                    outs["v"].append(vf.reshape(st["b"], st["s"], SWA_KV, SWA_HD))
            st["x1"], h2, lg = _out_ln(pieces, w_out, st["x"], m, ln1_g[l][None], ln1_b[l][None],
                                       wr, alpha=alpha, rows_per_mod=st["tiles_per_mod"] * ROW_TILE)
            st["xs"], st["gate"], st["pt"] = _route(lg, h2, tri, seq=st["s"], cap=st["cap"], nsb=st["nsb"],
                                                    n_exp=n_exp)

        yp, ys = _experts(streams["p"]["xs"], streams["s"]["xs"], streams["p"]["gate"], streams["s"]["gate"],
                          w_gate, w_up, w_down, l)
        for st, y in ((streams["p"], yp), (streams["s"], ys)):
            st["x"] = _combine(st["pt"], y, st["x1"], mods[l, st["mrow"]], ln2_g[l][None], ln2_b[l][None],
                               alpha=alpha, seq=st["s"], cap=st["cap"],
                               rows_per_mod=st["tiles_per_mod"] * ROW_TILE)

    y_prompt = streams["p"]["x"].reshape(bp, sp, d)
    y_sample = streams["s"]["x"].reshape(bs, ss, d)
    return (y_prompt, y_sample, jnp.stack(outs["ckv"], axis=1), jnp.stack(outs["kpe"], axis=1),
            jnp.stack(outs["k"], axis=1), jnp.stack(outs["v"], axis=1))
```

```python
import functools
import math

import numpy as np
import jax
import jax.numpy as jnp
from jax import lax
from jax.experimental import pallas as pl
from jax.experimental.pallas import tpu as pltpu

f32 = jnp.float32
bf16 = jnp.bfloat16

MLA_HEADS = 8
QK_NOPE = 128
QK_ROPE = 64
V_HEAD = 128
POOL_WINDOWS = (2, 4, 8, 16)
SWA_KV = 4
SWA_HD = 64
WINDOW = 128
CAP_FACTOR = 2
GRID_W = 64
ROPE_BASE = 10000.0
RMS_EPS = 1e-6
LN_EPS = 1e-5

LOG2E = 1.4426950408889634
LANES = 128
ROW_TILE = 256
MIB = 1024 * 1024


def _cparams(sem, vmem_mib):
    return pltpu.CompilerParams(dimension_semantics=sem, vmem_limit_bytes=vmem_mib * MIB)


def _dot(a, b):
    return jnp.dot(a, b, preferred_element_type=f32)


def _dot_nt(a, b):
    return lax.dot_general(a, b, (((1,), (1,)), ((), ())), preferred_element_type=f32)


def _silu(x):
    return x / (1.0 + jnp.exp(-x))


def _rope(r, c, sa, sb):
    return r * c + pltpu.roll(r, LANES - 1, axis=1) * sa + pltpu.roll(r, 1, axis=1) * sb


def _pipelined(n, produce, consume):
    produce(0)
    for i in range(n):
        if i + 1 < n:
            produce(i + 1)
        consume(i)


def _layer_norm(z, g, b):
    mu = jnp.mean(z, axis=-1, keepdims=True)
    zc = z - mu
    var = jnp.mean(zc * zc, axis=-1, keepdims=True)
    return zc * lax.rsqrt(var + LN_EPS) * g + b


def _mod_kernel(c_ref, w_ref, b_ref, o_ref):
    a = _silu(c_ref[...]).astype(bf16)
    o_ref[0] = _dot(a, w_ref[0].astype(bf16)) + b_ref[0]


def _modulation(cond, w_ada, b_ada):
    depth, d, n = w_ada.shape
    r = cond.shape[0]
    tn = 1024
    return pl.pallas_call(
        _mod_kernel,
        out_shape=jax.ShapeDtypeStruct((depth, r, n), f32),
        grid=(depth, n // tn),
        in_specs=[pl.BlockSpec((r, d), lambda l, j: (0, 0)),
                  pl.BlockSpec((1, d, tn), lambda l, j: (l, 0, j)),
                  pl.BlockSpec((1, 1, tn), lambda l, j: (l, 0, j))],
        out_specs=pl.BlockSpec((1, r, tn), lambda l, j: (l, 0, j)),
        compiler_params=_cparams(("arbitrary", "arbitrary"), 40),
        name="adaln_modulation",
    )(cond, w_ada, b_ada.reshape(depth, 1, n))


def _in_even_kernel(x_ref, mod_ref, wina_ref, winb_ref, qn_ref, kvn_ref, wuq_ref, wukv_ref, rc_ref, rsa_ref, rsb_ref,
                    q_ref, ckv_ref, kv_ref, kper_ref, kpea_ref, u_ref, proj_sc, *, rope, q_lora, kv_lora, qscale):
    sh = mod_ref[0, 0:1, :]
    sc = 1.0 + mod_ref[0, 1:2, :]
    o_kv = q_lora
    o_pe = q_lora + kv_lora
    o_u = o_pe + LANES
    sub = proj_sc.shape[1]

    def project(i):
        h = (x_ref[i * sub:(i + 1) * sub, :] * sc + sh).astype(bf16)
        pa = _dot(h, wina_ref[...])
        proj_sc[i % 2, :, 0:o_pe] = pa[:, 0:o_pe]
        kpe = pa[:, o_pe:]
        proj_sc[i % 2, :, o_pe:o_u] = jnp.concatenate([kpe, jnp.zeros((sub, LANES - kpe.shape[1]), f32)], axis=1)
        proj_sc[i % 2, :, o_u:] = _dot(h, winb_ref[...])

    def finish(i):
        rows = slice(i * sub, (i + 1) * sub)
        pq = proj_sc[i % 2, :, 0:o_kv]
        pkv = proj_sc[i % 2, :, o_kv:o_pe]
        kpe = proj_sc[i % 2, :, o_pe:o_u]
        u_ref[rows, :] = proj_sc[i % 2, :, o_u:]
        cq = pq * lax.rsqrt(jnp.mean(pq * pq, axis=-1, keepdims=True) + RMS_EPS) * qn_ref[...]
        ckv = pkv * lax.rsqrt(jnp.mean(pkv * pkv, axis=-1, keepdims=True) + RMS_EPS) * kvn_ref[...]
        ckv_ref[rows, :] = ckv
        kv_ref[rows, :] = _dot(ckv.astype(bf16), wukv_ref[...]).astype(bf16)
        kper_ref[rows, :] = kpe
        q = _dot(cq.astype(bf16), wuq_ref[...]) * qscale
        if rope:
            c, sa, sb = rc_ref[rows, :], rsa_ref[rows, :], rsb_ref[rows, :]
            kpea_ref[rows, :] = _rope(kpe, c, sa, sb).astype(bf16)
        else:
            kpea_ref[rows, :] = kpe.astype(bf16)
        for hd in range(MLA_HEADS):
            a = 2 * hd * LANES
            q_ref[rows, a:a + LANES] = q[:, a:a + LANES].astype(bf16)
            r = q[:, a + LANES:a + 2 * LANES]
            if rope:
                r = _rope(r, c, sa, sb)
            q_ref[rows, a + LANES:a + 2 * LANES] = r.astype(bf16)

    _pipelined(x_ref.shape[0] // sub, project, finish)


def _in_even(x, mods, w_in_a, w_in_b, qn, kvn, wuq, wukv, rtab, *, rope, rows_per_mod, seq):
    t, d = x.shape
    q_lora, kv_lora = qn.shape[1], kvn.shape[1]
    n_u = w_in_b.shape[1]
    n_in = q_lora + kv_lora + LANES + n_u
    nq, nkv = wuq.shape[1], wukv.shape[1]
    tm = ROW_TILE * SUB_TILES
    assert t % tm == 0 and rows_per_mod % tm == 0
    tiles_per_mod = rows_per_mod // tm
    tiles_per_seq = max(1, seq // tm)
    row = lambda i: (i, 0)
    const = lambda i: (0, 0)
    rmap = lambda i: (i % tiles_per_seq, 0)
    kern = functools.partial(_in_even_kernel, rope=rope, q_lora=q_lora, kv_lora=kv_lora,
                             qscale=(QK_NOPE + QK_ROPE) ** -0.5 * LOG2E)
    return pl.pallas_call(
        kern,
        out_shape=(jax.ShapeDtypeStruct((t, nq), bf16), jax.ShapeDtypeStruct((t, kv_lora), f32),
                   jax.ShapeDtypeStruct((t, nkv), bf16), jax.ShapeDtypeStruct((t, LANES), f32),
                   jax.ShapeDtypeStruct((t, LANES), bf16), jax.ShapeDtypeStruct((t, n_u), f32)),
        grid=(t // tm,),
        in_specs=[pl.BlockSpec((tm, d), row),
                  pl.BlockSpec((1, 6, d), lambda i: (i // tiles_per_mod, 0, 0)),
                  pl.BlockSpec(w_in_a.shape, const), pl.BlockSpec(w_in_b.shape, const),
                  pl.BlockSpec((1, q_lora), const), pl.BlockSpec((1, kv_lora), const),
                  pl.BlockSpec((q_lora, nq), const), pl.BlockSpec((kv_lora, nkv), const),
                  pl.BlockSpec((tm, LANES), rmap), pl.BlockSpec((tm, LANES), rmap), pl.BlockSpec((tm, LANES), rmap)],
        out_specs=(pl.BlockSpec((tm, nq), row), pl.BlockSpec((tm, kv_lora), row), pl.BlockSpec((tm, nkv), row),
                   pl.BlockSpec((tm, LANES), row), pl.BlockSpec((tm, LANES), row), pl.BlockSpec((tm, n_u), row)),
        scratch_shapes=[pltpu.VMEM((2, ROW_TILE, n_in), f32)],
        compiler_params=_cparams(("arbitrary",), 56),
        name="in_proj_even",
    )(x, mods, w_in_a, w_in_b, qn, kvn, wuq, wukv, *rtab)


def _mm_kernel(x_ref, w_ref, o_ref):
    o_ref[...] = _dot(x_ref[...].astype(bf16), w_ref[...]).astype(o_ref.dtype)


def _matmul(x, w, out_dtype, tm=512):
    m, k = x.shape
    n = w.shape[1]
    tm = min(tm, m)
    assert m % tm == 0
    return pl.pallas_call(
        _mm_kernel,
        out_shape=jax.ShapeDtypeStruct((m, n), out_dtype),
        grid=(m // tm,),
        in_specs=[pl.BlockSpec((tm, k), lambda i: (i, 0)), pl.BlockSpec((k, n), lambda i: (0, 0))],
        out_specs=pl.BlockSpec((tm, n), lambda i: (i, 0)),
        compiler_params=_cparams(("arbitrary",), 24),
        name="ctx_kv_up",
    )(x, w)


def _pool_kernel(u_ref, wp_ref, ps_ref, o_ref, *, seq):
    gc = wp_ref.shape[1]
    t = lax.broadcasted_iota(jnp.int32, (seq, 1), 0)

    def down(a, k):
        return jnp.where(t >= k, pltpu.roll(a, k, axis=0), 0.0)

    def up(a, k):
        return jnp.where(t < seq - k, pltpu.roll(a, seq - k, axis=0), 0.0)

    for gi, w in enumerate(POOL_WINDOWS):
        half = w // 2
        a = u_ref[:, gi * gc:(gi + 1) * gc]
        trail, lead, k = a, a, 1
        while k < half:
            trail = trail + down(trail, k)
            lead = lead + up(lead, k)
            k *= 2
        win = down(trail, 1) + lead
        cnt = (jnp.minimum(t + half, seq) - jnp.maximum(t - half, 0)).astype(f32)
        pooled = win / cnt - a
        mixed = _dot(pooled.astype(bf16), wp_ref[gi])
        o_ref[:, gi * gc:(gi + 1) * gc] = (mixed * ps_ref[:, gi * gc:(gi + 1) * gc]).astype(bf16)


def _pool(u, w_pool, pool_scale, seq):
    t, ch = u.shape
    g, gc, _ = w_pool.shape
    return pl.pallas_call(
        functools.partial(_pool_kernel, seq=seq),
        out_shape=jax.ShapeDtypeStruct((t, ch), bf16),
        grid=(t // seq,),
        in_specs=[pl.BlockSpec((seq, ch), lambda b: (b, 0)),
                  pl.BlockSpec((g, gc, gc), lambda b: (0, 0, 0)),
                  pl.BlockSpec((1, ch), lambda b: (0, 0))],
        out_specs=pl.BlockSpec((seq, ch), lambda b: (b, 0)),
        compiler_params=_cparams(("arbitrary",), 40),
        name="pool_mixer",
    )(u, w_pool, pool_scale)


def _mla_kernel(*refs, has_ctx):
    if has_ctx:
        q_ref, kv_ref, kpe_ref, kvc_ref, kpec_ref, o_ref, sc_sc = refs
    else:
        q_ref, kv_ref, kpe_ref, o_ref, sc_sc = refs
    seq = kv_ref.shape[0]

    def score(hd):
        a = 2 * hd * LANES
        q = q_ref[:, a:a + 2 * LANES]
        k = jnp.concatenate([kv_ref[:, a:a + LANES], kpe_ref[...]], axis=1)
        sc_sc[hd % 2, :, 0:seq] = _dot_nt(q, k)
        if has_ctx:
            kc = jnp.concatenate([kvc_ref[:, a:a + LANES], kpec_ref[...]], axis=1)
            sc_sc[hd % 2, :, seq:] = _dot_nt(q, kc)

    def attend(hd):
        a = 2 * hd * LANES
        x = sc_sc[hd % 2]
        m = x.max(axis=-1, keepdims=True)
        p = jnp.exp2(x - m)
        l = p.sum(axis=-1, keepdims=True)
        p = p.astype(bf16)
        o = _dot(p[:, 0:seq], kv_ref[:, a + LANES:a + 2 * LANES])
        if has_ctx:
            o = o + _dot(p[:, seq:], kvc_ref[:, a + LANES:a + 2 * LANES])
        o_ref[:, hd * LANES:(hd + 1) * LANES] = (o / l).astype(bf16)

    score(0)
    for hd in range(MLA_HEADS):
        if hd + 1 < MLA_HEADS:
            score(hd + 1)
        attend(hd)


def _mla_attention(q, kv, kpe, ctx, seq):
    t, nq = q.shape
    tq = ROW_TILE
    tps = seq // tq
    in_specs = [pl.BlockSpec((tq, nq), lambda b, j: (b * tps + j, 0)),
                pl.BlockSpec((seq, nq), lambda b, j: (b, 0)),
                pl.BlockSpec((seq, LANES), lambda b, j: (b, 0))]
    args = [q, kv, kpe]
    if ctx is not None:
        kvc, kpec, past = ctx
        in_specs += [pl.BlockSpec((past, nq), lambda b, j: (b, 0)),
                     pl.BlockSpec((past, LANES), lambda b, j: (b, 0))]
        args += [kvc, kpec]
    n_keys = seq + (ctx[2] if ctx is not None else 0)
    return pl.pallas_call(
        functools.partial(_mla_kernel, has_ctx=ctx is not None),
        out_shape=jax.ShapeDtypeStruct((t, MLA_HEADS * V_HEAD), bf16),
        grid=(t // seq, tps),
        in_specs=in_specs,
        out_specs=pl.BlockSpec((tq, MLA_HEADS * V_HEAD), lambda b, j: (b * tps + j, 0)),
        scratch_shapes=[pltpu.VMEM((2, tq, n_keys), f32)],
        compiler_params=_cparams(("arbitrary", "arbitrary"), 48),
        name="mla_attention",
    )(*args)


SUB_TILES = 2
SUB_ROWS = 128


def _out_ln_kernel(*refs, n_pieces, alpha):
    pieces = refs[:n_pieces]
    w_ref, x_ref, mod_ref, g_ref, b_ref, wr_ref, x1_ref, h2_ref, lg_ref, mix_sc = refs[n_pieces:]
    g1 = mod_ref[0, 2:3, :]
    sc2 = 1.0 + mod_ref[0, 4:5, :]
    sh2 = mod_ref[0, 3:4, :]
    sub = mix_sc.shape[1]

    def project(i):
        rows = slice(i * sub, (i + 1) * sub)
        off = 0
        mix = None
        for p in pieces:
            kp = p.shape[1]
            part = _dot(p[rows, :], w_ref[off:off + kp, :])
            mix = part if mix is None else mix + part
            off += kp
        mix_sc[i % 2] = mix

    def finish(i):
        rows = slice(i * sub, (i + 1) * sub)
        x1 = _layer_norm(alpha * x_ref[rows, :] + g1 * mix_sc[i % 2], g_ref[...], b_ref[...])
        x1_ref[rows, :] = x1
        h2 = x1 * sc2 + sh2
        h2h = h2.astype(bf16)
        h2_ref[rows, :] = h2h
        h2l = (h2 - h2h.astype(f32)).astype(bf16)
        hh = _dot(h2h, wr_ref[...])
        lg_ref[rows, :] = hh[:, 0:LANES] + hh[:, LANES:] + _dot(h2l, wr_ref[:, 0:LANES])

    _pipelined(x_ref.shape[0] // sub, project, finish)


def _out_ln(pieces, w_out, x, mods, ln_g, ln_b, wr, *, alpha, rows_per_mod):
    t, d = x.shape
    tm = ROW_TILE * SUB_TILES
    assert t % tm == 0 and rows_per_mod % tm == 0
    row = lambda i: (i, 0)
    const = lambda i: (0, 0)
    in_specs = [pl.BlockSpec((tm, p.shape[1]), row) for p in pieces]
    in_specs += [pl.BlockSpec(w_out.shape, const), pl.BlockSpec((tm, d), row),
                 pl.BlockSpec((1, 6, d), lambda i: (i // (rows_per_mod // tm), 0, 0)),
                 pl.BlockSpec((1, d), const), pl.BlockSpec((1, d), const),
                 pl.BlockSpec((d, 2 * LANES), const)]
    return pl.pallas_call(
        functools.partial(_out_ln_kernel, n_pieces=len(pieces), alpha=alpha),
        out_shape=(jax.ShapeDtypeStruct((t, d), f32), jax.ShapeDtypeStruct((t, d), bf16),
                   jax.ShapeDtypeStruct((t, LANES), f32)),
        grid=(t // tm,),
        in_specs=in_specs,
        out_specs=(pl.BlockSpec((tm, d), row), pl.BlockSpec((tm, d), row), pl.BlockSpec((tm, LANES), row)),
        scratch_shapes=[pltpu.VMEM((2, SUB_ROWS, d), f32)],
        compiler_params=_cparams(("arbitrary",), 56),
        name="out_proj_ln",
    )(*pieces, w_out, x, mods, ln_g, ln_b, wr)


SEARCH_BITS = 4
ROUTE_TOKENS = 1024
DISPATCH_ROWS = 512


def _kth_largest_bits(bits, cap):
    thr = jnp.zeros((bits.shape[0], 1), jnp.int32)
    hi = 31
    while hi > 0:
        lo = max(hi - SEARCH_BITS, 0) if hi % SEARCH_BITS == 0 else hi - hi % SEARCH_BITS
        digit = jnp.zeros_like(thr)
        for c in range(1, 1 << (hi - lo)):
            cnt = jnp.sum(jnp.where(bits >= (thr | (c << lo)), 1.0, 0.0), axis=1, keepdims=True)
            digit = digit + jnp.where(cnt >= cap, 1, 0)
        thr = thr | (digit * (1 << lo))
        hi = lo
    return thr


def _lane_prefix(mask, tri):
    n_rows, n = mask.shape
    blocks = [mask[:, k:k + LANES] for k in range(0, n, LANES)]
    local = _dot(jnp.concatenate(blocks, axis=0).astype(bf16), tri)
    out, carry = [], None
    for i in range(len(blocks)):
        blk = local[i * n_rows:(i + 1) * n_rows, :]
        out.append(blk if carry is None else blk + carry)
        total = blk[:, LANES - 1:LANES]
        carry = total if carry is None else carry + total
    return jnp.concatenate(out, axis=1)


def _route_kernel(lg_ref, h2_ref, tri_ref, xs_ref, gate_ref, pt_ref, aff_sc, pos_sc, *, cap, seq, n_exp, nsb):
    tri = tri_ref[...]
    pos_sc[...] = jnp.full(pos_sc.shape, -1.0, f32)
    for sb in range(nsb):
        lg = lg_ref[sb * seq:(sb + 1) * seq, :].T[0:n_exp, :]
        z = jnp.exp(lg - lg.max(axis=0, keepdims=True))
        aff = z / z.sum(axis=0, keepdims=True)
        bits = pltpu.bitcast(aff, jnp.int32)
        thr = _kth_largest_bits(bits, cap)
        gt = bits > thr
        eq = bits == thr
        need = cap - jnp.sum(jnp.where(gt, 1.0, 0.0), axis=1, keepdims=True)
        eq_rank = _lane_prefix(jnp.where(eq, 1.0, 0.0), tri)
        sel = gt | (eq & (eq_rank <= need))
        sel_rank = _lane_prefix(jnp.where(sel, 1.0, 0.0), tri)
        aff_sc[sb] = aff
        pos_sc[sb, 0:n_exp, :] = jnp.where(sel, sel_rank - 1.0, -1.0)

    slot_col = lax.broadcasted_iota(jnp.int32, (cap, seq), 0).astype(f32)

    eg = max(1, min(n_exp, DISPATCH_ROWS // cap))

    def dispatch(g, carry):
        for sb in range(nsb):
            rows = slice(sb * cap, (sb + 1) * cap)
            onehot = []
            for k in range(eg):
                e = g * eg + k
                hit = pos_sc[sb, pl.ds(e, 1), :] == slot_col
                onehot.append(jnp.where(hit, 1.0, 0.0).astype(bf16))
                gate = jnp.sum(jnp.where(hit, aff_sc[sb, pl.ds(e, 1), :], 0.0), axis=1, keepdims=True)
                gate_ref[e, rows, :] = jnp.broadcast_to(gate, (cap, LANES))
            picked = _dot(jnp.concatenate(onehot, axis=0), h2_ref[sb * seq:(sb + 1) * seq, :])
            for k in range(eg):
                xs_ref[g * eg + k, rows, :] = picked[k * cap:(k + 1) * cap, :].astype(bf16)
        return carry

    lax.fori_loop(0, n_exp // eg, dispatch, 0)

    lane = lax.broadcasted_iota(jnp.int32, (1, LANES), 1)
    for sb in range(nsb):
        pos_t = pos_sc[sb].T
        for j in range(n_exp * cap // LANES):
            if cap >= LANES:
                e = j * LANES // cap
                target = (lane + (j * LANES - e * cap)).astype(f32)
                hit = pos_t[:, e:e + 1] == target
            else:
                per = LANES // cap
                hit = None
                for k in range(per):
                    e = j * per + k
                    target = jnp.where((lane >= k * cap) & (lane < (k + 1) * cap), lane - k * cap, -7).astype(f32)
                    m = pos_t[:, e:e + 1] == target
                    hit = m if hit is None else hit | m
            pt_ref[sb * seq:(sb + 1) * seq, j * LANES:(j + 1) * LANES] = jnp.where(hit, 1.0, 0.0).astype(bf16)


def _route(lg, h2, tri, *, seq, cap, nsb, n_exp):
    t, d = h2.shape
    nb = t // seq
    assert nb % nsb == 0
    return pl.pallas_call(
        functools.partial(_route_kernel, cap=cap, seq=seq, n_exp=n_exp, nsb=nsb),
        out_shape=(jax.ShapeDtypeStruct((n_exp, nb * cap, d), bf16),
                   jax.ShapeDtypeStruct((n_exp, nb * cap, LANES), f32),
                   jax.ShapeDtypeStruct((t, n_exp * cap), bf16)),
        grid=(nb // nsb,),
        in_specs=[pl.BlockSpec((nsb * seq, LANES), lambda b: (b, 0)),
                  pl.BlockSpec((nsb * seq, d), lambda b: (b, 0)),
                  pl.BlockSpec((LANES, LANES), lambda b: (0, 0))],
        out_specs=(pl.BlockSpec((n_exp, nsb * cap, d), lambda b: (0, b, 0)),
                   pl.BlockSpec((n_exp, nsb * cap, LANES), lambda b: (0, b, 0)),
                   pl.BlockSpec((nsb * seq, n_exp * cap), lambda b: (b, 0))),
        scratch_shapes=[pltpu.VMEM((nsb, n_exp, seq), f32), pltpu.VMEM((nsb, LANES, seq), f32)],
        compiler_params=_cparams(("arbitrary",), 56),
        name="route_dispatch",
    )(lg, h2, tri)


EXPERT_TILE = 512
EXPERT_ROWS = 256


def _expert_kernel(xp_ref, xs_ref, gp_ref, gs_ref, wg_ref, wu_ref, wd_ref, yp_ref, ys_ref,
                   hid_sc, wa_sc, wb_sc, wc_sc, *, nf, tf):
    j = pl.program_id(1)
    mp, ms = xp_ref.shape[1], xs_ref.shape[1]
    cp, cs = min(EXPERT_ROWS, mp), min(EXPERT_ROWS, ms)
    chunks = [(xp_ref, gp_ref, yp_ref, r, r, cp) for r in range(0, mp, cp)]
    chunks += [(xs_ref, gs_ref, ys_ref, r, mp + r, cs) for r in range(0, ms, cs)]

    @pl.when(j < nf)
    def _():
        wa_sc[...] = wg_ref[0, 0].astype(bf16)
        wb_sc[...] = wu_ref[0, 0].astype(bf16)
        for x_ref, _, _, r, m, n in chunks:
            xc = x_ref[0, r:r + n, :]
            g = _dot(xc, wa_sc[...])
            u = _dot(xc, wb_sc[...])
            hid_sc[j, m:m + n, :] = (_silu(g) * u).astype(bf16)

    @pl.when(j >= nf)
    def _():
        wc_sc[...] = wd_ref[0, 0].astype(bf16)
        for _, g_ref, y_ref, r, m, n in chunks:
            acc = _dot(hid_sc[0, m:m + n, :], wc_sc[0:tf, :])
            for f in range(1, nf):
                acc = acc + _dot(hid_sc[f, m:m + n, :], wc_sc[f * tf:(f + 1) * tf, :])
            y_ref[0, r:r + n, :] = (acc * g_ref[0, r:r + n, 0:1]).astype(bf16)


def _experts(xp, xs, gp, gs, w_gate, w_up, w_down, layer):
    n_exp, mp, d = xp.shape
    ms = xs.shape[1]
    ff = w_gate.shape[3]
    tf = tn = EXPERT_TILE
    nf, nn = ff // tf, d // tn
    x_map = lambda e, j: (e, 0, 0)
    up_map = lambda e, j: (layer, e, 0, jnp.minimum(j, nf - 1))
    dn_map = lambda e, j: (layer, e, 0, jnp.maximum(j - nf, 0))
    y_map = lambda e, j: (e, 0, jnp.maximum(j - nf, 0))
    return pl.pallas_call(
        functools.partial(_expert_kernel, nf=nf, tf=tf),
        out_shape=(jax.ShapeDtypeStruct((n_exp, mp, d), bf16), jax.ShapeDtypeStruct((n_exp, ms, d), bf16)),
        grid=(n_exp, nf + nn),
        in_specs=[pl.BlockSpec((1, mp, d), x_map), pl.BlockSpec((1, ms, d), x_map),
                  pl.BlockSpec((1, mp, LANES), x_map), pl.BlockSpec((1, ms, LANES), x_map),
                  pl.BlockSpec((1, 1, d, tf), up_map), pl.BlockSpec((1, 1, d, tf), up_map),
                  pl.BlockSpec((1, 1, ff, tn), dn_map)],
        out_specs=(pl.BlockSpec((1, mp, tn), y_map), pl.BlockSpec((1, ms, tn), y_map)),
        scratch_shapes=[pltpu.VMEM((nf, mp + ms, tf), bf16), pltpu.VMEM((d, tf), bf16),
                        pltpu.VMEM((d, tf), bf16), pltpu.VMEM((ff, tn), bf16)],
        compiler_params=_cparams(("arbitrary", "arbitrary"), 58),
        name="expert_swiglu",
    )(xp, xs, gp, gs, w_gate, w_up, w_down)


def _combine_kernel(pt_ref, y_ref, x_ref, mod_ref, g_ref, b_ref, o_ref, ffn_sc, *, alpha, seq, cap):
    n_exp, _, d = y_ref.shape
    g2 = mod_ref[0, 5:6, :]
    sub = ffn_sc.shape[1]

    def scatter(i):
        sq = i * sub // seq
        y = y_ref[:, sq * cap:(sq + 1) * cap, :].reshape(n_exp * cap, d)
        ffn_sc[i % 2] = _dot(pt_ref[i * sub:(i + 1) * sub, :], y)

    def finish(i):
        rows = slice(i * sub, (i + 1) * sub)
        o_ref[rows, :] = _layer_norm(alpha * x_ref[rows, :] + g2 * ffn_sc[i % 2], g_ref[...], b_ref[...])

    _pipelined(x_ref.shape[0] // sub, scatter, finish)


def _combine(pt, y, x1, mods, ln_g, ln_b, *, alpha, seq, cap, rows_per_mod):
    t, d = x1.shape
    n_exp = y.shape[0]
    tm = ROW_TILE * SUB_TILES
    assert t % tm == 0 and rows_per_mod % tm == 0 and (tm % seq == 0 or seq % tm == 0)
    spt = max(1, tm // seq)
    tps = max(1, seq // tm)
    row = lambda i: (i, 0)
    const = lambda i: (0, 0)
    return pl.pallas_call(
        functools.partial(_combine_kernel, alpha=alpha, seq=seq, cap=cap),
        out_shape=jax.ShapeDtypeStruct((t, d), f32),
        grid=(t // tm,),
        in_specs=[pl.BlockSpec((tm, n_exp * cap), row),
                  pl.BlockSpec((n_exp, spt * cap, d), lambda i: (0, i // tps, 0)),
                  pl.BlockSpec((tm, d), row),
                  pl.BlockSpec((1, 6, d), lambda i: (i // (rows_per_mod // tm), 0, 0)),
                  pl.BlockSpec((1, d), const), pl.BlockSpec((1, d), const)],
        out_specs=pl.BlockSpec((tm, d), row),
        scratch_shapes=[pltpu.VMEM((2, ROW_TILE, d), f32)],
        compiler_params=_cparams(("arbitrary",), 56),
        name="combine_ln",
    )(pt, y, x1, mods, ln_g, ln_b)


def _pad_pairs(chunk, lane):
    swapped = pltpu.roll(chunk, LANES // 2, axis=1)
    lo = lane < LANES // 2
    return [jnp.where(lo, chunk, 0.0), jnp.where(lo, 0.0, swapped),
            jnp.where(lo, swapped, 0.0), jnp.where(lo, 0.0, chunk)]


def _in_odd_kernel(x_ref, mod_ref, win_ref, rc_ref, rsa_ref, rsb_ref,
                   q_ref, ka_ref, va_ref, kf_ref, vf_ref, proj_sc, *, rope, nq, nkv, qscale, tqa):
    sh = mod_ref[0, 0:1, :]
    sc = 1.0 + mod_ref[0, 1:2, :]
    pairs = nq // LANES // SWA_KV
    lane = lax.broadcasted_iota(jnp.int32, (1, LANES), 1)
    sub = proj_sc.shape[1]
    piece = min(sub, tqa)

    def project(i):
        h = (x_ref[i * sub:(i + 1) * sub, :] * sc + sh).astype(bf16)
        proj_sc[i % 2] = _dot(h, win_ref[...])

    def finish(i):
        rows = slice(i * sub, (i + 1) * sub)
        k = proj_sc[i % 2, :, nq:nq + nkv]
        v = proj_sc[i % 2, :, nq + nkv:nq + 2 * nkv]
        kf_ref[rows, :] = k
        vf_ref[rows, :] = v
        if rope:
            c, sa, sb = rc_ref[rows, :], rsa_ref[rows, :], rsb_ref[rows, :]
        for ci in range(nq // LANES):
            r = proj_sc[i % 2, :, ci * LANES:(ci + 1) * LANES]
            if rope:
                r = _rope(r, c, sa, sb)
            r = (r * qscale).astype(bf16)
            hk, pr = ci // pairs, ci % pairs
            for p0 in range(0, sub, piece):
                tok = i * sub + p0
                q_ref[tok // tqa, hk, pr * tqa + tok % tqa:pr * tqa + tok % tqa + piece, :] = r[p0:p0 + piece, :]
        for a in range(0, nkv, LANES):
            kc = k[:, a:a + LANES]
            if rope:
                kc = _rope(kc, c, sa, sb)
            for j, (kp, vp) in enumerate(zip(_pad_pairs(kc, lane), _pad_pairs(v[:, a:a + LANES], lane))):
                o = 4 * a + j * LANES
                ka_ref[rows, o:o + LANES] = kp.astype(bf16)
                va_ref[rows, o:o + LANES] = vp.astype(bf16)

    _pipelined(x_ref.shape[0] // sub, project, finish)


def _in_odd(x, mods, w_in, rtab, *, rope, tqa, rows_per_mod, seq):
    t, d = x.shape
    n_in = w_in.shape[1]
    nkv = SWA_KV * SWA_HD
    nq = n_in - 2 * nkv
    tm = ROW_TILE * SUB_TILES
    assert tm % tqa == 0 and ROW_TILE % tqa == 0 and t % tm == 0 and rows_per_mod % tm == 0
    tiles_per_mod = rows_per_mod // tm
    tiles_per_seq = max(1, seq // tm)
    q_rows = nq // LANES // SWA_KV * tqa
    row = lambda i: (i, 0)
    rmap = lambda i: (i % tiles_per_seq, 0)
    kern = functools.partial(_in_odd_kernel, rope=rope, nq=nq, nkv=nkv, qscale=SWA_HD ** -0.5 * LOG2E, tqa=tqa)
    return pl.pallas_call(
        kern,
        out_shape=(jax.ShapeDtypeStruct((t // tqa, SWA_KV, q_rows, LANES), bf16),
                   jax.ShapeDtypeStruct((t, 4 * nkv), bf16),
                   jax.ShapeDtypeStruct((t, 4 * nkv), bf16), jax.ShapeDtypeStruct((t, nkv), f32),
                   jax.ShapeDtypeStruct((t, nkv), f32)),
        grid=(t // tm,),
        in_specs=[pl.BlockSpec((tm, d), row),
                  pl.BlockSpec((1, 6, d), lambda i: (i // tiles_per_mod, 0, 0)),
                  pl.BlockSpec((d, n_in), lambda i: (0, 0)),
                  pl.BlockSpec((tm, LANES), rmap), pl.BlockSpec((tm, LANES), rmap), pl.BlockSpec((tm, LANES), rmap)],
        out_specs=(pl.BlockSpec((tm // tqa, SWA_KV, q_rows, LANES), lambda i: (i, 0, 0, 0)),
                   pl.BlockSpec((tm, 4 * nkv), row), pl.BlockSpec((tm, 4 * nkv), row),
                   pl.BlockSpec((tm, nkv), row), pl.BlockSpec((tm, nkv), row)),
        scratch_shapes=[pltpu.VMEM((2, ROW_TILE, n_in), f32)],
        compiler_params=_cparams(("arbitrary",), 56),
        name="in_proj_odd",
    )(x, mods, w_in, *rtab)


GQA_BLOCKS = 4


def _gqa_kernel(*refs, windowed, seq, tq):
    if windowed:
        q_ref, ka_ref, va_ref, kc_ref, vc_ref, sink_ref, o_ref, sc_sc = refs
    else:
        q_ref, ka_ref, va_ref, sink_ref, o_ref, sc_sc = refs
    nblk = q_ref.shape[0]
    pairs = q_ref.shape[2] // tq
    rows, bias = [], []
    for blk in range(nblk):
        if windowed:
            span = tq + 2 * WINDOW
            q0 = (pl.program_id(1) * nblk + blk) * tq
            start = pl.multiple_of(jnp.clip(q0 - WINDOW, 0, seq - span), LANES)
            kpos = start + lax.broadcasted_iota(jnp.int32, (tq, span), 1)
            qpos = q0 + lax.broadcasted_iota(jnp.int32, (tq, span), 0)
            bias1 = jnp.where(jnp.abs(kpos - qpos) <= WINDOW, 0.0, -1e30)
            bias.append(jnp.concatenate([bias1] * pairs, axis=0))
            rows.append(pl.ds(start, span))
        else:
            rows.append(slice(blk * seq, (blk + 1) * seq))
    n_loc = tq + 2 * WINDOW if windowed else seq
    n_units = 2 * SWA_KV

    def chunk(u):
        return slice(u * LANES, (u + 1) * LANES)

    def score(k):
        blk, u = divmod(k, n_units)
        q = q_ref[blk, u // 2]
        sc = _dot_nt(q, ka_ref[rows[blk], chunk(u)])
        if windowed:
            sc_sc[k % 2, :, 0:n_loc] = sc + bias[blk]
            sc_sc[k % 2, :, n_loc:] = _dot_nt(q, kc_ref[:, chunk(u)])
        else:
            sc_sc[k % 2] = sc

    halves = {}

    def attend(k):
        blk, u = divmod(k, n_units)
        heads = [2 * (u // 2 * pairs + pr) + u % 2 for pr in range(pairs)]
        sink = jnp.concatenate([jnp.broadcast_to(sink_ref[h:h + 1, 0:1], (tq, 1)) for h in heads], axis=0) * LOG2E
        x = sc_sc[k % 2]
        m = jnp.maximum(sink, x.max(axis=-1, keepdims=True))
        p = jnp.exp2(x - m)
        l = jnp.exp2(sink - m) + p.sum(axis=-1, keepdims=True)
        p = p.astype(bf16)
        o = _dot(p[:, 0:n_loc], va_ref[rows[blk], chunk(u)])
        if windowed:
            o = o + _dot(p[:, n_loc:], vc_ref[:, chunk(u)])
        o = o / l
        if u % 2 == 0:
            halves[blk] = o
        else:
            out = halves.pop(blk) + o
            for pr in range(pairs):
                col = (u // 2 * pairs + pr) * LANES
                o_ref[blk * tq:(blk + 1) * tq, col:col + LANES] = out[pr * tq:(pr + 1) * tq, :].astype(bf16)

    _pipelined(nblk * n_units, score, attend)


def _gqa_attention(q, ka, va, sink, ctx, *, seq, windowed):
    t, nk = ka.shape
    tq = WINDOW if windowed else seq
    tps = seq // tq
    nb = t // seq
    q_rows = q.shape[2]
    nq = q_rows // tq * SWA_KV * LANES
    nblk = 1 if windowed else math.gcd(GQA_BLOCKS, nb)
    seqs = 1 if windowed else nblk
    steps = tps // nblk if windowed else 1
    in_specs = [pl.BlockSpec((nblk, SWA_KV, q_rows, LANES), lambda b, j: (b * steps + j, 0, 0, 0)),
                pl.BlockSpec((seqs * seq, nk), lambda b, j: (b, 0)),
                pl.BlockSpec((seqs * seq, nk), lambda b, j: (b, 0))]
    args = [q, ka, va]
    if windowed:
        kc, vc, past = ctx
        in_specs += [pl.BlockSpec((past, nk), lambda b, j: (b, 0)), pl.BlockSpec((past, nk), lambda b, j: (b, 0))]
        args += [kc, vc]
    in_specs.append(pl.BlockSpec(sink.shape, lambda b, j: (0, 0)))
    args.append(sink)
    n_keys = tq + 2 * WINDOW + ctx[2] if windowed else seq
    return pl.pallas_call(
        functools.partial(_gqa_kernel, windowed=windowed, seq=seq, tq=tq),
        out_shape=jax.ShapeDtypeStruct((t, nq), bf16),
        grid=(nb // seqs, steps),
        in_specs=in_specs,
        out_specs=pl.BlockSpec((nblk * tq, nq), lambda b, j: (b * steps + j, 0)),
        scratch_shapes=[pltpu.VMEM((2, q_rows, n_keys), f32)],
        compiler_params=_cparams(("arbitrary", "arbitrary"), 56),
        name="gqa_attention",
    )(*args)


def _rope_tables(n_tokens, rot_dim):
    rows = n_tokens // GRID_W
    row = jnp.repeat(jnp.arange(rows, dtype=f32), GRID_W)
    col = jnp.tile(jnp.arange(GRID_W, dtype=f32), rows)
    n_freq = rot_dim // 4
    inv = ROPE_BASE ** (-jnp.arange(n_freq, dtype=f32) / n_freq)
    ang = jnp.concatenate([row[:, None] * inv, col[:, None] * inv], -1)
    cos = jnp.repeat(jnp.cos(ang), 2, axis=1)
    sin = jnp.repeat(jnp.sin(ang), 2, axis=1)
    even = (jnp.arange(rot_dim) % 2 == 0)[None, :]
    sa = jnp.where(even, -sin, 0.0)
    sb = jnp.where(even, 0.0, sin)
    reps = LANES // rot_dim
    if reps >= 1:
        return tuple(jnp.tile(a, (1, reps)) for a in (cos, sa, sb))
    raise NotImplementedError


def _pad_kv_pairs(a):
    rows = a.shape[0]
    a = a.reshape(rows, SWA_KV, SWA_HD)
    z = jnp.zeros_like(a)
    return jnp.concatenate([a, z, z, a], axis=-1).reshape(rows, SWA_KV * 4 * SWA_HD)


def _split_hi_lo(w):
    hi = w.astype(bf16)
    return hi, (w - hi.astype(f32)).astype(bf16)


def kernel(x_prompt, x_sample, cache_mla_ckv, cache_mla_kpe, cache_swa_k, cache_swa_v, c, c_ctx, w_ada, b_ada, w_in_even, q_norm, kv_norm, w_uq, w_ukv, w_pool, pool_scale, w_out_even, w_in_odd, sink, w_out_odd, ln1_g, ln1_b, w_router, w_gate, w_up, w_down, ln2_g, ln2_b):
    depth, d = w_ada.shape[0], w_ada.shape[1]
    bp, sp, _ = x_prompt.shape
    bs, ss, _ = x_sample.shape
    past = cache_mla_ckv.shape[2]
    n_exp = w_router.shape[2]
    alpha = (2 * depth) ** 0.25
    q_lora, kv_lora = q_norm.shape[1], kv_norm.shape[1]
    pool_ch = pool_scale.shape[1]
    assert sp % ROW_TILE == 0 and ss % ROW_TILE == 0 and ss % GRID_W == 0
    assert QK_ROPE <= LANES and LANES % SWA_HD == 0 and 2 * SWA_HD == LANES

    n_rows = -(-(1 + bs) // 16) * 16
    cond = jnp.zeros((n_rows, d), f32).at[0].set(c_ctx).at[1:1 + bs].set(c)
    mods = _modulation(cond, w_ada, b_ada)
    mods = mods.reshape(depth, n_rows, 6, d)

    streams = {
        "p": dict(x=x_prompt.reshape(bp * sp, d), b=bp, s=sp, rope=False, mrow=slice(0, 1),
                  tiles_per_mod=bp * sp // ROW_TILE),
        "s": dict(x=x_sample.reshape(bs * ss, d), b=bs, s=ss, rope=True, mrow=slice(1, 1 + bs),
                  tiles_per_mod=ss // ROW_TILE),
    }
    for st in streams.values():
        st["cap"] = max(1, CAP_FACTOR * st["s"] // n_exp)
        st["nsb"] = math.gcd(st["b"], max(1, ROUTE_TOKENS // st["s"]))
        assert (n_exp * st["cap"]) % LANES == 0 and (LANES % st["cap"] == 0 or st["cap"] % LANES == 0)
        m_rows = st["b"] * st["cap"]
        assert m_rows % min(EXPERT_ROWS, m_rows) == 0 and m_rows % 16 == 0

    tri = jnp.triu(jnp.ones((LANES, LANES), bf16))
    rope_mla = _rope_tables(ss, QK_ROPE)
    rope_swa = _rope_tables(ss, SWA_HD)
    ident = (jnp.ones((ROW_TILE * SUB_TILES, LANES), f32),) + (jnp.zeros((ROW_TILE * SUB_TILES, LANES), f32),) * 2

    outs = {"ckv": [], "kpe": [], "k": [], "v": []}
    for l in range(depth):
        wr = jnp.concatenate(_split_hi_lo(jnp.pad(w_router[l], ((0, 0), (0, LANES - n_exp)))), axis=1)
        if l % 2 == 0:
            e = l // 2
            o_pool = q_lora + kv_lora + QK_ROPE
            w_in_a = w_in_even[e, :, :o_pool].astype(bf16)
            w_in_b = w_in_even[e, :, o_pool:].astype(bf16)
            wq = w_uq[e].reshape(q_lora, MLA_HEADS, QK_NOPE + QK_ROPE)
            wq = jnp.pad(wq, ((0, 0), (0, 0), (0, 2 * LANES - QK_NOPE - QK_ROPE)))
            wq = wq.reshape(q_lora, MLA_HEADS * 2 * LANES).astype(bf16)
            wkv = w_ukv[e].astype(bf16)
            w_out = w_out_even[e].astype(bf16)
            wp = w_pool[e].astype(bf16)
            kvc = _matmul(cache_mla_ckv[:, e].reshape(bs * past, kv_lora), wkv, bf16)
            kpec = jnp.pad(cache_mla_kpe[:, e].reshape(bs * past, QK_ROPE),
                           ((0, 0), (0, LANES - QK_ROPE))).astype(bf16)
        else:
            o = l // 2
            w_in = w_in_odd[o].astype(bf16)
            w_out = w_out_odd[o].astype(bf16)
            sink_b = jnp.broadcast_to(sink[o][:, None], (sink.shape[1], LANES))
            kc = _pad_kv_pairs(cache_swa_k[:, o].reshape(bs * past, SWA_KV * SWA_HD)).astype(bf16)
            vc = _pad_kv_pairs(cache_swa_v[:, o].reshape(bs * past, SWA_KV * SWA_HD)).astype(bf16)

        for name, st in streams.items():
            m = mods[l, st["mrow"]]
            common = dict(rope=st["rope"], rows_per_mod=st["tiles_per_mod"] * ROW_TILE, seq=st["s"])
            if l % 2 == 0:
                q, ckv, kv, kpe_raw, kpe_att, u = _in_even(
                    st["x"], m, w_in_a, w_in_b, q_norm[e][None], kv_norm[e][None], wq, wkv,
                    rope_mla if st["rope"] else ident, **common)
                pooled = _pool(u, wp, pool_scale[e][None], st["s"])
                ctx = (kvc, kpec, past) if st["rope"] else None
                attn = _mla_attention(q, kv, kpe_att, ctx, st["s"])
                pieces = [attn, pooled]
                if not st["rope"]:
                    outs["ckv"].append(ckv.reshape(st["b"], st["s"], kv_lora))
                    outs["kpe"].append(kpe_raw[:, :QK_ROPE].reshape(st["b"], st["s"], QK_ROPE))
            else:
                q, ka, va, kf, vf = _in_odd(st["x"], m, w_in, rope_swa if st["rope"] else ident,
                                            tqa=WINDOW if st["rope"] else st["s"], **common)
                ctx = (kc, vc, past) if st["rope"] else None
                attn = _gqa_attention(q, ka, va, sink_b, ctx, seq=st["s"], windowed=st["rope"])
                pieces = [attn]
                if not st["rope"]:
                    outs["k"].append(kf.reshape(st["b"], st["s"], SWA_KV, SWA_HD))
                    outs["v"].append(vf.reshape(st["b"], st["s"], SWA_KV, SWA_HD))
            st["x1"], h2, lg = _out_ln(pieces, w_out, st["x"], m, ln1_g[l][None], ln1_b[l][None],
                                       wr, alpha=alpha, rows_per_mod=st["tiles_per_mod"] * ROW_TILE)
            st["xs"], st["gate"], st["pt"] = _route(lg, h2, tri, seq=st["s"], cap=st["cap"], nsb=st["nsb"],
                                                    n_exp=n_exp)

        yp, ys = _experts(streams["p"]["xs"], streams["s"]["xs"], streams["p"]["gate"], streams["s"]["gate"],
                          w_gate, w_up, w_down, l)
        for st, y in ((streams["p"], yp), (streams["s"], ys)):
            st["x"] = _combine(st["pt"], y, st["x1"], mods[l, st["mrow"]], ln2_g[l][None], ln2_b[l][None],
                               alpha=alpha, seq=st["s"], cap=st["cap"],
                               rows_per_mod=st["tiles_per_mod"] * ROW_TILE)

    y_prompt = streams["p"]["x"].reshape(bp, sp, d)
    y_sample = streams["s"]["x"].reshape(bs, ss, d)
    return (y_prompt, y_sample, jnp.stack(outs["ckv"], axis=1), jnp.stack(outs["kpe"], axis=1),
            jnp.stack(outs["k"], axis=1), jnp.stack(outs["v"], axis=1))
```

```python
import functools
import math

import numpy as np
import jax
import jax.numpy as jnp
from jax import lax
from jax.experimental import pallas as pl
from jax.experimental.pallas import tpu as pltpu

f32 = jnp.float32
bf16 = jnp.bfloat16

MLA_HEADS = 8
QK_NOPE = 128
QK_ROPE = 64
V_HEAD = 128
POOL_WINDOWS = (2, 4, 8, 16)
SWA_KV = 4
SWA_HD = 64
WINDOW = 128
CAP_FACTOR = 2
GRID_W = 64
ROPE_BASE = 10000.0
RMS_EPS = 1e-6
LN_EPS = 1e-5

LOG2E = 1.4426950408889634
LANES = 128
ROW_TILE = 256
MIB = 1024 * 1024


def _cparams(sem, vmem_mib):
    return pltpu.CompilerParams(dimension_semantics=sem, vmem_limit_bytes=vmem_mib * MIB)


def _dot(a, b):
    return jnp.dot(a, b, preferred_element_type=f32)


def _dot_nt(a, b):
    return lax.dot_general(a, b, (((1,), (1,)), ((), ())), preferred_element_type=f32)


def _silu(x):
    return x / (1.0 + jnp.exp(-x))


def _rope(r, c, sa, sb):
    return r * c + pltpu.roll(r, LANES - 1, axis=1) * sa + pltpu.roll(r, 1, axis=1) * sb


def _pipelined(n, produce, consume):
    produce(0)
    for i in range(n):
        if i + 1 < n:
            produce(i + 1)
        consume(i)


def _layer_norm(z, g, b):
    mu = jnp.mean(z, axis=-1, keepdims=True)
    zc = z - mu
    var = jnp.mean(zc * zc, axis=-1, keepdims=True)
    return zc * lax.rsqrt(var + LN_EPS) * g + b


def _mod_kernel(c_ref, w_ref, b_ref, o_ref):
    a = _silu(c_ref[...]).astype(bf16)
    o_ref[0] = _dot(a, w_ref[0].astype(bf16)) + b_ref[0]


def _modulation(cond, w_ada, b_ada):
    depth, d, n = w_ada.shape
    r = cond.shape[0]
    tn = 1024
    return pl.pallas_call(
        _mod_kernel,
        out_shape=jax.ShapeDtypeStruct((depth, r, n), f32),
        grid=(depth, n // tn),
        in_specs=[pl.BlockSpec((r, d), lambda l, j: (0, 0)),
                  pl.BlockSpec((1, d, tn), lambda l, j: (l, 0, j)),
                  pl.BlockSpec((1, 1, tn), lambda l, j: (l, 0, j))],
        out_specs=pl.BlockSpec((1, r, tn), lambda l, j: (l, 0, j)),
        compiler_params=_cparams(("arbitrary", "arbitrary"), 40),
        name="adaln_modulation",
    )(cond, w_ada, b_ada.reshape(depth, 1, n))


def _in_even_kernel(x_ref, mod_ref, wina_ref, winb_ref, qn_ref, kvn_ref, wuq_ref, wukv_ref, rc_ref, rsa_ref, rsb_ref,
                    q_ref, ckv_ref, kv_ref, kper_ref, kpea_ref, u_ref, proj_sc, *, rope, q_lora, kv_lora, qscale):
    sh = mod_ref[0, 0:1, :]
    sc = 1.0 + mod_ref[0, 1:2, :]
    o_kv = q_lora
    o_pe = q_lora + kv_lora
    o_u = o_pe + LANES
    sub = proj_sc.shape[1]

    def project(i):
        h = (x_ref[i * sub:(i + 1) * sub, :] * sc + sh).astype(bf16)
        pa = _dot(h, wina_ref[...])
        proj_sc[i % 2, :, 0:o_pe] = pa[:, 0:o_pe]
        kpe = pa[:, o_pe:]
        proj_sc[i % 2, :, o_pe:o_u] = jnp.concatenate([kpe, jnp.zeros((sub, LANES - kpe.shape[1]), f32)], axis=1)
        proj_sc[i % 2, :, o_u:] = _dot(h, winb_ref[...])

    def finish(i):
        rows = slice(i * sub, (i + 1) * sub)
        pq = proj_sc[i % 2, :, 0:o_kv]
        pkv = proj_sc[i % 2, :, o_kv:o_pe]
        kpe = proj_sc[i % 2, :, o_pe:o_u]
        u_ref[rows, :] = proj_sc[i % 2, :, o_u:]
        cq = pq * lax.rsqrt(jnp.mean(pq * pq, axis=-1, keepdims=True) + RMS_EPS) * qn_ref[...]
        ckv = pkv * lax.rsqrt(jnp.mean(pkv * pkv, axis=-1, keepdims=True) + RMS_EPS) * kvn_ref[...]
        ckv_ref[rows, :] = ckv
        kv_ref[rows, :] = _dot(ckv.astype(bf16), wukv_ref[...]).astype(bf16)
        kper_ref[rows, :] = kpe
        q = _dot(cq.astype(bf16), wuq_ref[...]) * qscale
        if rope:
            c, sa, sb = rc_ref[rows, :], rsa_ref[rows, :], rsb_ref[rows, :]
            kpea_ref[rows, :] = _rope(kpe, c, sa, sb).astype(bf16)
        else:
            kpea_ref[rows, :] = kpe.astype(bf16)
        for hd in range(MLA_HEADS):
            a = 2 * hd * LANES
            q_ref[rows, a:a + LANES] = q[:, a:a + LANES].astype(bf16)
            r = q[:, a + LANES:a + 2 * LANES]
            if rope:
                r = _rope(r, c, sa, sb)
            q_ref[rows, a + LANES:a + 2 * LANES] = r.astype(bf16)

    _pipelined(x_ref.shape[0] // sub, project, finish)


def _in_even(x, mods, w_in_a, w_in_b, qn, kvn, wuq, wukv, rtab, *, rope, rows_per_mod, seq):
    t, d = x.shape
    q_lora, kv_lora = qn.shape[1], kvn.shape[1]
    n_u = w_in_b.shape[1]
    n_in = q_lora + kv_lora + LANES + n_u
    nq, nkv = wuq.shape[1], wukv.shape[1]
    tm = ROW_TILE * SUB_TILES
    assert t % tm == 0 and rows_per_mod % tm == 0
    tiles_per_mod = rows_per_mod // tm
    tiles_per_seq = max(1, seq // tm)
    row = lambda i: (i, 0)
    const = lambda i: (0, 0)
    rmap = lambda i: (i % tiles_per_seq, 0)
    kern = functools.partial(_in_even_kernel, rope=rope, q_lora=q_lora, kv_lora=kv_lora,
                             qscale=(QK_NOPE + QK_ROPE) ** -0.5 * LOG2E)
    return pl.pallas_call(
        kern,
        out_shape=(jax.ShapeDtypeStruct((t, nq), bf16), jax.ShapeDtypeStruct((t, kv_lora), f32),
                   jax.ShapeDtypeStruct((t, nkv), bf16), jax.ShapeDtypeStruct((t, LANES), f32),
                   jax.ShapeDtypeStruct((t, LANES), bf16), jax.ShapeDtypeStruct((t, n_u), f32)),
        grid=(t // tm,),
        in_specs=[pl.BlockSpec((tm, d), row),
                  pl.BlockSpec((1, 6, d), lambda i: (i // tiles_per_mod, 0, 0)),
                  pl.BlockSpec(w_in_a.shape, const), pl.BlockSpec(w_in_b.shape, const),
                  pl.BlockSpec((1, q_lora), const), pl.BlockSpec((1, kv_lora), const),
                  pl.BlockSpec((q_lora, nq), const), pl.BlockSpec((kv_lora, nkv), const),
                  pl.BlockSpec((tm, LANES), rmap), pl.BlockSpec((tm, LANES), rmap), pl.BlockSpec((tm, LANES), rmap)],
        out_specs=(pl.BlockSpec((tm, nq), row), pl.BlockSpec((tm, kv_lora), row), pl.BlockSpec((tm, nkv), row),
                   pl.BlockSpec((tm, LANES), row), pl.BlockSpec((tm, LANES), row), pl.BlockSpec((tm, n_u), row)),
        scratch_shapes=[pltpu.VMEM((2, ROW_TILE, n_in), f32)],
        compiler_params=_cparams(("arbitrary",), 56),
        name="in_proj_even",
    )(x, mods, w_in_a, w_in_b, qn, kvn, wuq, wukv, *rtab)


def _mm_kernel(x_ref, w_ref, o_ref):
    o_ref[...] = _dot(x_ref[...].astype(bf16), w_ref[...]).astype(o_ref.dtype)


def _matmul(x, w, out_dtype, tm=512):
    m, k = x.shape
    n = w.shape[1]
    tm = min(tm, m)
    assert m % tm == 0
    return pl.pallas_call(
        _mm_kernel,
        out_shape=jax.ShapeDtypeStruct((m, n), out_dtype),
        grid=(m // tm,),
        in_specs=[pl.BlockSpec((tm, k), lambda i: (i, 0)), pl.BlockSpec((k, n), lambda i: (0, 0))],
        out_specs=pl.BlockSpec((tm, n), lambda i: (i, 0)),
        compiler_params=_cparams(("arbitrary",), 24),
        name="ctx_kv_up",
    )(x, w)


def _pool_kernel(u_ref, wp_ref, ps_ref, o_ref, *, seq):
    gc = wp_ref.shape[1]
    t = lax.broadcasted_iota(jnp.int32, (seq, 1), 0)

    def down(a, k):
        return jnp.where(t >= k, pltpu.roll(a, k, axis=0), 0.0)

    def up(a, k):
        return jnp.where(t < seq - k, pltpu.roll(a, seq - k, axis=0), 0.0)

    for gi, w in enumerate(POOL_WINDOWS):
        half = w // 2
        a = u_ref[:, gi * gc:(gi + 1) * gc]
        trail, lead, k = a, a, 1
        while k < half:
            trail = trail + down(trail, k)
            lead = lead + up(lead, k)
            k *= 2
        win = down(trail, 1) + lead
        cnt = (jnp.minimum(t + half, seq) - jnp.maximum(t - half, 0)).astype(f32)
        pooled = win / cnt - a
        mixed = _dot(pooled.astype(bf16), wp_ref[gi])
        o_ref[:, gi * gc:(gi + 1) * gc] = (mixed * ps_ref[:, gi * gc:(gi + 1) * gc]).astype(bf16)


def _pool(u, w_pool, pool_scale, seq):
    t, ch = u.shape
    g, gc, _ = w_pool.shape
    return pl.pallas_call(
        functools.partial(_pool_kernel, seq=seq),
        out_shape=jax.ShapeDtypeStruct((t, ch), bf16),
        grid=(t // seq,),
        in_specs=[pl.BlockSpec((seq, ch), lambda b: (b, 0)),
                  pl.BlockSpec((g, gc, gc), lambda b: (0, 0, 0)),
                  pl.BlockSpec((1, ch), lambda b: (0, 0))],
        out_specs=pl.BlockSpec((seq, ch), lambda b: (b, 0)),
        compiler_params=_cparams(("arbitrary",), 40),
        name="pool_mixer",
    )(u, w_pool, pool_scale)


def _mla_kernel(*refs, has_ctx):
    if has_ctx:
        q_ref, kv_ref, kpe_ref, kvc_ref, kpec_ref, o_ref, sc_sc = refs
    else:
        q_ref, kv_ref, kpe_ref, o_ref, sc_sc = refs
    seq = kv_ref.shape[0]

    def score(hd):
        a = 2 * hd * LANES
        q = q_ref[:, a:a + 2 * LANES]
        k = jnp.concatenate([kv_ref[:, a:a + LANES], kpe_ref[...]], axis=1)
        sc_sc[hd % 2, :, 0:seq] = _dot_nt(q, k)
        if has_ctx:
            kc = jnp.concatenate([kvc_ref[:, a:a + LANES], kpec_ref[...]], axis=1)
            sc_sc[hd % 2, :, seq:] = _dot_nt(q, kc)

    def attend(hd):
        a = 2 * hd * LANES
        x = sc_sc[hd % 2]
        m = x.max(axis=-1, keepdims=True)
        p = jnp.exp2(x - m)
        l = p.sum(axis=-1, keepdims=True)
        p = p.astype(bf16)
        o = _dot(p[:, 0:seq], kv_ref[:, a + LANES:a + 2 * LANES])
        if has_ctx:
            o = o + _dot(p[:, seq:], kvc_ref[:, a + LANES:a + 2 * LANES])
        o_ref[:, hd * LANES:(hd + 1) * LANES] = (o / l).astype(bf16)

    score(0)
    for hd in range(MLA_HEADS):
        if hd + 1 < MLA_HEADS:
            score(hd + 1)
        attend(hd)


def _mla_attention(q, kv, kpe, ctx, seq):
    t, nq = q.shape
    tq = ROW_TILE
    tps = seq // tq
    in_specs = [pl.BlockSpec((tq, nq), lambda b, j: (b * tps + j, 0)),
                pl.BlockSpec((seq, nq), lambda b, j: (b, 0)),
                pl.BlockSpec((seq, LANES), lambda b, j: (b, 0))]
    args = [q, kv, kpe]
    if ctx is not None:
        kvc, kpec, past = ctx
        in_specs += [pl.BlockSpec((past, nq), lambda b, j: (b, 0)),
                     pl.BlockSpec((past, LANES), lambda b, j: (b, 0))]
        args += [kvc, kpec]
    n_keys = seq + (ctx[2] if ctx is not None else 0)
    return pl.pallas_call(
        functools.partial(_mla_kernel, has_ctx=ctx is not None),
        out_shape=jax.ShapeDtypeStruct((t, MLA_HEADS * V_HEAD), bf16),
        grid=(t // seq, tps),
        in_specs=in_specs,
        out_specs=pl.BlockSpec((tq, MLA_HEADS * V_HEAD), lambda b, j: (b * tps + j, 0)),
        scratch_shapes=[pltpu.VMEM((2, tq, n_keys), f32)],
        compiler_params=_cparams(("arbitrary", "arbitrary"), 48),
        name="mla_attention",
    )(*args)


SUB_TILES = 2
SUB_ROWS = 128


def _out_ln_kernel(*refs, n_pieces, alpha):
    pieces = refs[:n_pieces]
    w_ref, x_ref, mod_ref, g_ref, b_ref, wr_ref, x1_ref, h2_ref, lg_ref, mix_sc = refs[n_pieces:]
    g1 = mod_ref[0, 2:3, :]
    sc2 = 1.0 + mod_ref[0, 4:5, :]
    sh2 = mod_ref[0, 3:4, :]
    sub = mix_sc.shape[1]

    def project(i):
        rows = slice(i * sub, (i + 1) * sub)
        off = 0
        mix = None
        for p in pieces:
            kp = p.shape[1]
            part = _dot(p[rows, :], w_ref[off:off + kp, :])
            mix = part if mix is None else mix + part
            off += kp
        mix_sc[i % 2] = mix

    def finish(i):
        rows = slice(i * sub, (i + 1) * sub)
        x1 = _layer_norm(alpha * x_ref[rows, :] + g1 * mix_sc[i % 2], g_ref[...], b_ref[...])
        x1_ref[rows, :] = x1
        h2 = x1 * sc2 + sh2
        h2h = h2.astype(bf16)
        h2_ref[rows, :] = h2h
        h2l = (h2 - h2h.astype(f32)).astype(bf16)
        hh = _dot(h2h, wr_ref[...])
        lg_ref[rows, :] = hh[:, 0:LANES] + hh[:, LANES:] + _dot(h2l, wr_ref[:, 0:LANES])

    _pipelined(x_ref.shape[0] // sub, project, finish)


def _out_ln(pieces, w_out, x, mods, ln_g, ln_b, wr, *, alpha, rows_per_mod):
    t, d = x.shape
    tm = ROW_TILE * SUB_TILES
    assert t % tm == 0 and rows_per_mod % tm == 0
    row = lambda i: (i, 0)
    const = lambda i: (0, 0)
    in_specs = [pl.BlockSpec((tm, p.shape[1]), row) for p in pieces]
    in_specs += [pl.BlockSpec(w_out.shape, const), pl.BlockSpec((tm, d), row),
                 pl.BlockSpec((1, 6, d), lambda i: (i // (rows_per_mod // tm), 0, 0)),
                 pl.BlockSpec((1, d), const), pl.BlockSpec((1, d), const),
                 pl.BlockSpec((d, 2 * LANES), const)]
    return pl.pallas_call(
        functools.partial(_out_ln_kernel, n_pieces=len(pieces), alpha=alpha),
        out_shape=(jax.ShapeDtypeStruct((t, d), f32), jax.ShapeDtypeStruct((t, d), bf16),
                   jax.ShapeDtypeStruct((t, LANES), f32)),
        grid=(t // tm,),
        in_specs=in_specs,
        out_specs=(pl.BlockSpec((tm, d), row), pl.BlockSpec((tm, d), row), pl.BlockSpec((tm, LANES), row)),
        scratch_shapes=[pltpu.VMEM((2, SUB_ROWS, d), f32)],
        compiler_params=_cparams(("arbitrary",), 56),
        name="out_proj_ln",
    )(*pieces, w_out, x, mods, ln_g, ln_b, wr)


SEARCH_BITS = 3
ROUTE_TOKENS = 1024
DISPATCH_ROWS = 512


def _kth_largest_bits(bits, cap):
    thr = jnp.zeros((bits.shape[0], 1), jnp.int32)
    hi = 31
    while hi > 0:
        lo = max(hi - SEARCH_BITS, 0) if hi % SEARCH_BITS == 0 else hi - hi % SEARCH_BITS
        digit = jnp.zeros_like(thr)
        for c in range(1, 1 << (hi - lo)):
            cnt = jnp.sum(jnp.where(bits >= (thr | (c << lo)), 1.0, 0.0), axis=1, keepdims=True)
            digit = digit + jnp.where(cnt >= cap, 1, 0)
        thr = thr | (digit * (1 << lo))
        hi = lo
    return thr


def _lane_prefix(mask, tri):
    n_rows, n = mask.shape
    blocks = [mask[:, k:k + LANES] for k in range(0, n, LANES)]
    local = _dot(jnp.concatenate(blocks, axis=0).astype(bf16), tri)
    out, carry = [], None
    for i in range(len(blocks)):
        blk = local[i * n_rows:(i + 1) * n_rows, :]
        out.append(blk if carry is None else blk + carry)
        total = blk[:, LANES - 1:LANES]
        carry = total if carry is None else carry + total
    return jnp.concatenate(out, axis=1)


def _route_kernel(lg_ref, h2_ref, tri_ref, xs_ref, gate_ref, pt_ref, aff_sc, pos_sc, *, cap, seq, n_exp, nsb):
    tri = tri_ref[...]
    pos_sc[...] = jnp.full(pos_sc.shape, -1.0, f32)
    for sb in range(nsb):
        lg = lg_ref[sb * seq:(sb + 1) * seq, :].T[0:n_exp, :]
        z = jnp.exp(lg - lg.max(axis=0, keepdims=True))
        aff = z / z.sum(axis=0, keepdims=True)
        bits = pltpu.bitcast(aff, jnp.int32)
        thr = _kth_largest_bits(bits, cap)
        gt = bits > thr
        eq = bits == thr
        need = cap - jnp.sum(jnp.where(gt, 1.0, 0.0), axis=1, keepdims=True)
        eq_rank = _lane_prefix(jnp.where(eq, 1.0, 0.0), tri)
        sel = gt | (eq & (eq_rank <= need))
        sel_rank = _lane_prefix(jnp.where(sel, 1.0, 0.0), tri)
        aff_sc[sb] = aff
        pos_sc[sb, 0:n_exp, :] = jnp.where(sel, sel_rank - 1.0, -1.0)

    slot_col = lax.broadcasted_iota(jnp.int32, (cap, seq), 0).astype(f32)

    eg = max(1, min(n_exp, DISPATCH_ROWS // cap))

    def dispatch(g, carry):
        for sb in range(nsb):
            rows = slice(sb * cap, (sb + 1) * cap)
            onehot = []
            for k in range(eg):
                e = g * eg + k
                hit = pos_sc[sb, pl.ds(e, 1), :] == slot_col
                onehot.append(jnp.where(hit, 1.0, 0.0).astype(bf16))
                gate = jnp.sum(jnp.where(hit, aff_sc[sb, pl.ds(e, 1), :], 0.0), axis=1, keepdims=True)
                gate_ref[e, rows, :] = jnp.broadcast_to(gate, (cap, LANES))
            picked = _dot(jnp.concatenate(onehot, axis=0), h2_ref[sb * seq:(sb + 1) * seq, :])
            for k in range(eg):
                xs_ref[g * eg + k, rows, :] = picked[k * cap:(k + 1) * cap, :].astype(bf16)
        return carry

    lax.fori_loop(0, n_exp // eg, dispatch, 0)

    lane = lax.broadcasted_iota(jnp.int32, (1, LANES), 1)
    for sb in range(nsb):
        pos_t = pos_sc[sb].T
        for j in range(n_exp * cap // LANES):
            if cap >= LANES:
                e = j * LANES // cap
                target = (lane + (j * LANES - e * cap)).astype(f32)
                hit = pos_t[:, e:e + 1] == target
            else:
                per = LANES // cap
                hit = None
                for k in range(per):
                    e = j * per + k
                    target = jnp.where((lane >= k * cap) & (lane < (k + 1) * cap), lane - k * cap, -7).astype(f32)
                    m = pos_t[:, e:e + 1] == target
                    hit = m if hit is None else hit | m
            pt_ref[sb * seq:(sb + 1) * seq, j * LANES:(j + 1) * LANES] = jnp.where(hit, 1.0, 0.0).astype(bf16)


def _route(lg, h2, tri, *, seq, cap, nsb, n_exp):
    t, d = h2.shape
    nb = t // seq
    assert nb % nsb == 0
    return pl.pallas_call(
        functools.partial(_route_kernel, cap=cap, seq=seq, n_exp=n_exp, nsb=nsb),
        out_shape=(jax.ShapeDtypeStruct((n_exp, nb * cap, d), bf16),
                   jax.ShapeDtypeStruct((n_exp, nb * cap, LANES), f32),
                   jax.ShapeDtypeStruct((t, n_exp * cap), bf16)),
        grid=(nb // nsb,),
        in_specs=[pl.BlockSpec((nsb * seq, LANES), lambda b: (b, 0)),
                  pl.BlockSpec((nsb * seq, d), lambda b: (b, 0)),
                  pl.BlockSpec((LANES, LANES), lambda b: (0, 0))],
        out_specs=(pl.BlockSpec((n_exp, nsb * cap, d), lambda b: (0, b, 0)),
                   pl.BlockSpec((n_exp, nsb * cap, LANES), lambda b: (0, b, 0)),
                   pl.BlockSpec((nsb * seq, n_exp * cap), lambda b: (b, 0))),
        scratch_shapes=[pltpu.VMEM((nsb, n_exp, seq), f32), pltpu.VMEM((nsb, LANES, seq), f32)],
        compiler_params=_cparams(("arbitrary",), 56),
        name="route_dispatch",
    )(lg, h2, tri)


EXPERT_TILE = 512
EXPERT_ROWS = 256


def _expert_kernel(xp_ref, xs_ref, gp_ref, gs_ref, wg_ref, wu_ref, wd_ref, yp_ref, ys_ref,
                   hid_sc, wa_sc, wb_sc, wc_sc, *, nf, tf):
    j = pl.program_id(1)
    mp, ms = xp_ref.shape[1], xs_ref.shape[1]
    cp, cs = min(EXPERT_ROWS, mp), min(EXPERT_ROWS, ms)
    chunks = [(xp_ref, gp_ref, yp_ref, r, r, cp) for r in range(0, mp, cp)]
    chunks += [(xs_ref, gs_ref, ys_ref, r, mp + r, cs) for r in range(0, ms, cs)]

    @pl.when(j < nf)
    def _():
        wa_sc[...] = wg_ref[0, 0].astype(bf16)
        wb_sc[...] = wu_ref[0, 0].astype(bf16)
        for x_ref, _, _, r, m, n in chunks:
            xc = x_ref[0, r:r + n, :]
            g = _dot(xc, wa_sc[...])
            u = _dot(xc, wb_sc[...])
            hid_sc[j, m:m + n, :] = (_silu(g) * u).astype(bf16)

    @pl.when(j >= nf)
    def _():
        wc_sc[...] = wd_ref[0, 0].astype(bf16)
        for _, g_ref, y_ref, r, m, n in chunks:
            acc = _dot(hid_sc[0, m:m + n, :], wc_sc[0:tf, :])
            for f in range(1, nf):
                acc = acc + _dot(hid_sc[f, m:m + n, :], wc_sc[f * tf:(f + 1) * tf, :])
            y_ref[0, r:r + n, :] = (acc * g_ref[0, r:r + n, 0:1]).astype(bf16)


def _experts(xp, xs, gp, gs, w_gate, w_up, w_down, layer):
    n_exp, mp, d = xp.shape
    ms = xs.shape[1]
    ff = w_gate.shape[3]
    tf = tn = EXPERT_TILE
    nf, nn = ff // tf, d // tn
    x_map = lambda e, j: (e, 0, 0)
    up_map = lambda e, j: (layer, e, 0, jnp.minimum(j, nf - 1))
    dn_map = lambda e, j: (layer, e, 0, jnp.maximum(j - nf, 0))
    y_map = lambda e, j: (e, 0, jnp.maximum(j - nf, 0))
    return pl.pallas_call(
        functools.partial(_expert_kernel, nf=nf, tf=tf),
        out_shape=(jax.ShapeDtypeStruct((n_exp, mp, d), bf16), jax.ShapeDtypeStruct((n_exp, ms, d), bf16)),
        grid=(n_exp, nf + nn),
        in_specs=[pl.BlockSpec((1, mp, d), x_map), pl.BlockSpec((1, ms, d), x_map),
                  pl.BlockSpec((1, mp, LANES), x_map), pl.BlockSpec((1, ms, LANES), x_map),
                  pl.BlockSpec((1, 1, d, tf), up_map), pl.BlockSpec((1, 1, d, tf), up_map),
                  pl.BlockSpec((1, 1, ff, tn), dn_map)],
        out_specs=(pl.BlockSpec((1, mp, tn), y_map), pl.BlockSpec((1, ms, tn), y_map)),
        scratch_shapes=[pltpu.VMEM((nf, mp + ms, tf), bf16), pltpu.VMEM((d, tf), bf16),
                        pltpu.VMEM((d, tf), bf16), pltpu.VMEM((ff, tn), bf16)],
        compiler_params=_cparams(("arbitrary", "arbitrary"), 58),
        name="expert_swiglu",
    )(xp, xs, gp, gs, w_gate, w_up, w_down)


def _combine_kernel(pt_ref, y_ref, x_ref, mod_ref, g_ref, b_ref, o_ref, ffn_sc, *, alpha, seq, cap):
    n_exp, _, d = y_ref.shape
    g2 = mod_ref[0, 5:6, :]
    sub = ffn_sc.shape[1]

    def scatter(i):
        sq = i * sub // seq
        y = y_ref[:, sq * cap:(sq + 1) * cap, :].reshape(n_exp * cap, d)
        ffn_sc[i % 2] = _dot(pt_ref[i * sub:(i + 1) * sub, :], y)

    def finish(i):
        rows = slice(i * sub, (i + 1) * sub)
        o_ref[rows, :] = _layer_norm(alpha * x_ref[rows, :] + g2 * ffn_sc[i % 2], g_ref[...], b_ref[...])

    _pipelined(x_ref.shape[0] // sub, scatter, finish)


def _combine(pt, y, x1, mods, ln_g, ln_b, *, alpha, seq, cap, rows_per_mod):
    t, d = x1.shape
    n_exp = y.shape[0]
    tm = ROW_TILE * SUB_TILES
    assert t % tm == 0 and rows_per_mod % tm == 0 and (tm % seq == 0 or seq % tm == 0)
    spt = max(1, tm // seq)
    tps = max(1, seq // tm)
    row = lambda i: (i, 0)
    const = lambda i: (0, 0)
    return pl.pallas_call(
        functools.partial(_combine_kernel, alpha=alpha, seq=seq, cap=cap),
        out_shape=jax.ShapeDtypeStruct((t, d), f32),
        grid=(t // tm,),
        in_specs=[pl.BlockSpec((tm, n_exp * cap), row),
                  pl.BlockSpec((n_exp, spt * cap, d), lambda i: (0, i // tps, 0)),
                  pl.BlockSpec((tm, d), row),
                  pl.BlockSpec((1, 6, d), lambda i: (i // (rows_per_mod // tm), 0, 0)),
                  pl.BlockSpec((1, d), const), pl.BlockSpec((1, d), const)],
        out_specs=pl.BlockSpec((tm, d), row),
        scratch_shapes=[pltpu.VMEM((2, ROW_TILE, d), f32)],
        compiler_params=_cparams(("arbitrary",), 56),
        name="combine_ln",
    )(pt, y, x1, mods, ln_g, ln_b)


def _pad_pairs(chunk, lane):
    swapped = pltpu.roll(chunk, LANES // 2, axis=1)
    lo = lane < LANES // 2
    return [jnp.where(lo, chunk, 0.0), jnp.where(lo, 0.0, swapped),
            jnp.where(lo, swapped, 0.0), jnp.where(lo, 0.0, chunk)]


def _in_odd_kernel(x_ref, mod_ref, win_ref, rc_ref, rsa_ref, rsb_ref,
                   q_ref, ka_ref, va_ref, kf_ref, vf_ref, proj_sc, *, rope, nq, nkv, qscale, tqa):
    sh = mod_ref[0, 0:1, :]
    sc = 1.0 + mod_ref[0, 1:2, :]
    pairs = nq // LANES // SWA_KV
    lane = lax.broadcasted_iota(jnp.int32, (1, LANES), 1)
    sub = proj_sc.shape[1]
    piece = min(sub, tqa)

    def project(i):
        h = (x_ref[i * sub:(i + 1) * sub, :] * sc + sh).astype(bf16)
        proj_sc[i % 2] = _dot(h, win_ref[...])

    def finish(i):
        rows = slice(i * sub, (i + 1) * sub)
        k = proj_sc[i % 2, :, nq:nq + nkv]
        v = proj_sc[i % 2, :, nq + nkv:nq + 2 * nkv]
        kf_ref[rows, :] = k
        vf_ref[rows, :] = v
        if rope:
            c, sa, sb = rc_ref[rows, :], rsa_ref[rows, :], rsb_ref[rows, :]
        for ci in range(nq // LANES):
            r = proj_sc[i % 2, :, ci * LANES:(ci + 1) * LANES]
            if rope:
                r = _rope(r, c, sa, sb)
            r = (r * qscale).astype(bf16)
            hk, pr = ci // pairs, ci % pairs
            for p0 in range(0, sub, piece):
                tok = i * sub + p0
                q_ref[tok // tqa, hk, pr * tqa + tok % tqa:pr * tqa + tok % tqa + piece, :] = r[p0:p0 + piece, :]
        for a in range(0, nkv, LANES):
            kc = k[:, a:a + LANES]
            if rope:
                kc = _rope(kc, c, sa, sb)
            for j, (kp, vp) in enumerate(zip(_pad_pairs(kc, lane), _pad_pairs(v[:, a:a + LANES], lane))):
                o = 4 * a + j * LANES
                ka_ref[rows, o:o + LANES] = kp.astype(bf16)
                va_ref[rows, o:o + LANES] = vp.astype(bf16)

    _pipelined(x_ref.shape[0] // sub, project, finish)


def _in_odd(x, mods, w_in, rtab, *, rope, tqa, rows_per_mod, seq):
    t, d = x.shape
    n_in = w_in.shape[1]
    nkv = SWA_KV * SWA_HD
    nq = n_in - 2 * nkv
    tm = ROW_TILE * SUB_TILES
    assert tm % tqa == 0 and ROW_TILE % tqa == 0 and t % tm == 0 and rows_per_mod % tm == 0
    tiles_per_mod = rows_per_mod // tm
    tiles_per_seq = max(1, seq // tm)
    q_rows = nq // LANES // SWA_KV * tqa
    row = lambda i: (i, 0)
    rmap = lambda i: (i % tiles_per_seq, 0)
    kern = functools.partial(_in_odd_kernel, rope=rope, nq=nq, nkv=nkv, qscale=SWA_HD ** -0.5 * LOG2E, tqa=tqa)
    return pl.pallas_call(
        kern,
        out_shape=(jax.ShapeDtypeStruct((t // tqa, SWA_KV, q_rows, LANES), bf16),
                   jax.ShapeDtypeStruct((t, 4 * nkv), bf16),
                   jax.ShapeDtypeStruct((t, 4 * nkv), bf16), jax.ShapeDtypeStruct((t, nkv), f32),
                   jax.ShapeDtypeStruct((t, nkv), f32)),
        grid=(t // tm,),
        in_specs=[pl.BlockSpec((tm, d), row),
                  pl.BlockSpec((1, 6, d), lambda i: (i // tiles_per_mod, 0, 0)),
                  pl.BlockSpec((d, n_in), lambda i: (0, 0)),
                  pl.BlockSpec((tm, LANES), rmap), pl.BlockSpec((tm, LANES), rmap), pl.BlockSpec((tm, LANES), rmap)],
        out_specs=(pl.BlockSpec((tm // tqa, SWA_KV, q_rows, LANES), lambda i: (i, 0, 0, 0)),
                   pl.BlockSpec((tm, 4 * nkv), row), pl.BlockSpec((tm, 4 * nkv), row),
                   pl.BlockSpec((tm, nkv), row), pl.BlockSpec((tm, nkv), row)),
        scratch_shapes=[pltpu.VMEM((2, ROW_TILE, n_in), f32)],
        compiler_params=_cparams(("arbitrary",), 56),
        name="in_proj_odd",
    )(x, mods, w_in, *rtab)


GQA_BLOCKS = 4


def _gqa_kernel(*refs, windowed, seq, tq):
    if windowed:
        q_ref, ka_ref, va_ref, kcc_ref, vcc_ref, sink_ref, o_ref, sc_sc, kc_ref, vc_ref = refs

        @pl.when(pl.program_id(1) == 0)
        def _():
            lane = lax.broadcasted_iota(jnp.int32, (1, LANES), 1)
            for a in range(0, kcc_ref.shape[1], LANES):
                kparts = _pad_pairs(kcc_ref[:, a:a + LANES].astype(f32), lane)
                vparts = _pad_pairs(vcc_ref[:, a:a + LANES].astype(f32), lane)
                for i in range(4):
                    o = 4 * a + i * LANES
                    kc_ref[:, o:o + LANES] = kparts[i].astype(bf16)
                    vc_ref[:, o:o + LANES] = vparts[i].astype(bf16)
    else:
        q_ref, ka_ref, va_ref, sink_ref, o_ref, sc_sc = refs
    nblk = q_ref.shape[0]
    pairs = q_ref.shape[2] // tq
    rows, bias = [], []
    for blk in range(nblk):
        if windowed:
            span = tq + 2 * WINDOW
            q0 = (pl.program_id(1) * nblk + blk) * tq
            start = pl.multiple_of(jnp.clip(q0 - WINDOW, 0, seq - span), LANES)
            kpos = start + lax.broadcasted_iota(jnp.int32, (tq, span), 1)
            qpos = q0 + lax.broadcasted_iota(jnp.int32, (tq, span), 0)
            bias1 = jnp.where(jnp.abs(kpos - qpos) <= WINDOW, 0.0, -1e30)
            bias.append(jnp.concatenate([bias1] * pairs, axis=0))
            rows.append(pl.ds(start, span))
        else:
            rows.append(slice(blk * seq, (blk + 1) * seq))
    n_loc = tq + 2 * WINDOW if windowed else seq
    n_units = 2 * SWA_KV

    def chunk(u):
        return slice(u * LANES, (u + 1) * LANES)

    def score(k):
        blk, u = divmod(k, n_units)
        q = q_ref[blk, u // 2]
        sc = _dot_nt(q, ka_ref[rows[blk], chunk(u)])
        if windowed:
            sc_sc[k % 2, :, 0:n_loc] = sc + bias[blk]
            sc_sc[k % 2, :, n_loc:] = _dot_nt(q, kc_ref[:, chunk(u)])
        else:
            sc_sc[k % 2] = sc

    halves = {}

    def attend(k):
        blk, u = divmod(k, n_units)
        heads = [2 * (u // 2 * pairs + pr) + u % 2 for pr in range(pairs)]
        sink = jnp.concatenate([jnp.broadcast_to(sink_ref[h:h + 1, 0:1], (tq, 1)) for h in heads], axis=0) * LOG2E
        x = sc_sc[k % 2]
        m = jnp.maximum(sink, x.max(axis=-1, keepdims=True))
        p = jnp.exp2(x - m)
        l = jnp.exp2(sink - m) + p.sum(axis=-1, keepdims=True)
        p = p.astype(bf16)
        o = _dot(p[:, 0:n_loc], va_ref[rows[blk], chunk(u)])
        if windowed:
            o = o + _dot(p[:, n_loc:], vc_ref[:, chunk(u)])
        o = o / l
        if u % 2 == 0:
            halves[blk] = o
        else:
            out = halves.pop(blk) + o
            for pr in range(pairs):
                col = (u // 2 * pairs + pr) * LANES
                o_ref[blk * tq:(blk + 1) * tq, col:col + LANES] = out[pr * tq:(pr + 1) * tq, :].astype(bf16)

    _pipelined(nblk * n_units, score, attend)


def _gqa_attention(q, ka, va, sink, ctx, *, seq, windowed):
    t, nk = ka.shape
    tq = WINDOW if windowed else seq
    tps = seq // tq
    nb = t // seq
    q_rows = q.shape[2]
    nq = q_rows // tq * SWA_KV * LANES
    nblk = 1 if windowed else math.gcd(GQA_BLOCKS, nb)
    seqs = 1 if windowed else nblk
    steps = tps // nblk if windowed else 1
    in_specs = [pl.BlockSpec((nblk, SWA_KV, q_rows, LANES), lambda b, j: (b * steps + j, 0, 0, 0)),
                pl.BlockSpec((seqs * seq, nk), lambda b, j: (b, 0)),
                pl.BlockSpec((seqs * seq, nk), lambda b, j: (b, 0))]
    args = [q, ka, va]
    scratch = []
    if windowed:
        kc, vc, past = ctx
        in_specs += [pl.BlockSpec((past, kc.shape[1]), lambda b, j: (b, 0)),
                     pl.BlockSpec((past, vc.shape[1]), lambda b, j: (b, 0))]
        args += [kc, vc]
        scratch = [pltpu.VMEM((past, nk), bf16), pltpu.VMEM((past, nk), bf16)]
    in_specs.append(pl.BlockSpec(sink.shape, lambda b, j: (0, 0)))
    args.append(sink)
    n_keys = tq + 2 * WINDOW + ctx[2] if windowed else seq
    return pl.pallas_call(
        functools.partial(_gqa_kernel, windowed=windowed, seq=seq, tq=tq),
        out_shape=jax.ShapeDtypeStruct((t, nq), bf16),
        grid=(nb // seqs, steps),
        in_specs=in_specs,
        out_specs=pl.BlockSpec((nblk * tq, nq), lambda b, j: (b * steps + j, 0)),
        scratch_shapes=[pltpu.VMEM((2, q_rows, n_keys), f32)] + scratch,
        compiler_params=_cparams(("arbitrary", "arbitrary"), 56),
        name="gqa_attention",
    )(*args)


def _rope_tables(n_tokens, rot_dim):
    rows = n_tokens // GRID_W
    row = jnp.repeat(jnp.arange(rows, dtype=f32), GRID_W)
    col = jnp.tile(jnp.arange(GRID_W, dtype=f32), rows)
    n_freq = rot_dim // 4
    inv = ROPE_BASE ** (-jnp.arange(n_freq, dtype=f32) / n_freq)
    ang = jnp.concatenate([row[:, None] * inv, col[:, None] * inv], -1)
    cos = jnp.repeat(jnp.cos(ang), 2, axis=1)
    sin = jnp.repeat(jnp.sin(ang), 2, axis=1)
    even = (jnp.arange(rot_dim) % 2 == 0)[None, :]
    sa = jnp.where(even, -sin, 0.0)
    sb = jnp.where(even, 0.0, sin)
    reps = LANES // rot_dim
    if reps >= 1:
        return tuple(jnp.tile(a, (1, reps)) for a in (cos, sa, sb))
    raise NotImplementedError


def _split_hi_lo(w):
    hi = w.astype(bf16)
    return hi, (w - hi.astype(f32)).astype(bf16)


def kernel(x_prompt, x_sample, cache_mla_ckv, cache_mla_kpe, cache_swa_k, cache_swa_v, c, c_ctx, w_ada, b_ada, w_in_even, q_norm, kv_norm, w_uq, w_ukv, w_pool, pool_scale, w_out_even, w_in_odd, sink, w_out_odd, ln1_g, ln1_b, w_router, w_gate, w_up, w_down, ln2_g, ln2_b):
    depth, d = w_ada.shape[0], w_ada.shape[1]
    bp, sp, _ = x_prompt.shape
    bs, ss, _ = x_sample.shape
    past = cache_mla_ckv.shape[2]
    n_exp = w_router.shape[2]
    alpha = (2 * depth) ** 0.25
    q_lora, kv_lora = q_norm.shape[1], kv_norm.shape[1]
    pool_ch = pool_scale.shape[1]
    assert sp % ROW_TILE == 0 and ss % ROW_TILE == 0 and ss % GRID_W == 0
    assert QK_ROPE <= LANES and LANES % SWA_HD == 0 and 2 * SWA_HD == LANES

    n_rows = -(-(1 + bs) // 16) * 16
    cond = jnp.zeros((n_rows, d), f32).at[0].set(c_ctx).at[1:1 + bs].set(c)
    mods = _modulation(cond, w_ada, b_ada)
    mods = mods.reshape(depth, n_rows, 6, d)

    streams = {
        "p": dict(x=x_prompt.reshape(bp * sp, d), b=bp, s=sp, rope=False, mrow=slice(0, 1),
                  tiles_per_mod=bp * sp // ROW_TILE),
        "s": dict(x=x_sample.reshape(bs * ss, d), b=bs, s=ss, rope=True, mrow=slice(1, 1 + bs),
                  tiles_per_mod=ss // ROW_TILE),
    }
    for st in streams.values():
        st["cap"] = max(1, CAP_FACTOR * st["s"] // n_exp)
        st["nsb"] = math.gcd(st["b"], max(1, ROUTE_TOKENS // st["s"]))
        assert (n_exp * st["cap"]) % LANES == 0 and (LANES % st["cap"] == 0 or st["cap"] % LANES == 0)
        m_rows = st["b"] * st["cap"]
        assert m_rows % min(EXPERT_ROWS, m_rows) == 0 and m_rows % 16 == 0

    tri = jnp.triu(jnp.ones((LANES, LANES), bf16))
    rope_mla = _rope_tables(ss, QK_ROPE)
    rope_swa = _rope_tables(ss, SWA_HD)
    ident = (jnp.ones((ROW_TILE * SUB_TILES, LANES), f32),) + (jnp.zeros((ROW_TILE * SUB_TILES, LANES), f32),) * 2

    outs = {"ckv": [], "kpe": [], "k": [], "v": []}
    for l in range(depth):
        wr = jnp.concatenate(_split_hi_lo(jnp.pad(w_router[l], ((0, 0), (0, LANES - n_exp)))), axis=1)
        if l % 2 == 0:
            e = l // 2
            o_pool = q_lora + kv_lora + QK_ROPE
            w_in_a = w_in_even[e, :, :o_pool].astype(bf16)
            w_in_b = w_in_even[e, :, o_pool:].astype(bf16)
            wq = w_uq[e].reshape(q_lora, MLA_HEADS, QK_NOPE + QK_ROPE)
            wq = jnp.pad(wq, ((0, 0), (0, 0), (0, 2 * LANES - QK_NOPE - QK_ROPE)))
            wq = wq.reshape(q_lora, MLA_HEADS * 2 * LANES).astype(bf16)
            wkv = w_ukv[e].astype(bf16)
            w_out = w_out_even[e].astype(bf16)
            wp = w_pool[e].astype(bf16)
            kvc = _matmul(cache_mla_ckv[:, e].reshape(bs * past, kv_lora), wkv, bf16)
            kpec = jnp.pad(cache_mla_kpe[:, e].reshape(bs * past, QK_ROPE),
                           ((0, 0), (0, LANES - QK_ROPE))).astype(bf16)
        else:
            o = l // 2
            w_in = w_in_odd[o].astype(bf16)
            w_out = w_out_odd[o].astype(bf16)
            sink_b = jnp.broadcast_to(sink[o][:, None], (sink.shape[1], LANES))
            kc = cache_swa_k[:, o].reshape(bs * past, SWA_KV * SWA_HD).astype(bf16)
            vc = cache_swa_v[:, o].reshape(bs * past, SWA_KV * SWA_HD).astype(bf16)

        for name, st in streams.items():
            m = mods[l, st["mrow"]]
            common = dict(rope=st["rope"], rows_per_mod=st["tiles_per_mod"] * ROW_TILE, seq=st["s"])
            if l % 2 == 0:
                q, ckv, kv, kpe_raw, kpe_att, u = _in_even(
                    st["x"], m, w_in_a, w_in_b, q_norm[e][None], kv_norm[e][None], wq, wkv,
                    rope_mla if st["rope"] else ident, **common)
                pooled = _pool(u, wp, pool_scale[e][None], st["s"])
                ctx = (kvc, kpec, past) if st["rope"] else None
                attn = _mla_attention(q, kv, kpe_att, ctx, st["s"])
                pieces = [attn, pooled]
                if not st["rope"]:
                    outs["ckv"].append(ckv.reshape(st["b"], st["s"], kv_lora))
                    outs["kpe"].append(kpe_raw[:, :QK_ROPE].reshape(st["b"], st["s"], QK_ROPE))
            else:
                q, ka, va, kf, vf = _in_odd(st["x"], m, w_in, rope_swa if st["rope"] else ident,
                                            tqa=WINDOW if st["rope"] else st["s"], **common)
                ctx = (kc, vc, past) if st["rope"] else None
                attn = _gqa_attention(q, ka, va, sink_b, ctx, seq=st["s"], windowed=st["rope"])
                pieces = [attn]
                if not st["rope"]:
                    outs["k"].append(kf.reshape(st["b"], st["s"], SWA_KV, SWA_HD))
                    outs["v"].append(vf.reshape(st["b"], st["s"], SWA_KV, SWA_HD))
            st["x1"], h2, lg = _out_ln(pieces, w_out, st["x"], m, ln1_g[l][None], ln1_b[l][None],
                                       wr, alpha=alpha, rows_per_mod=st["tiles_per_mod"] * ROW_TILE)
            st["xs"], st["gate"], st["pt"] = _route(lg, h2, tri, seq=st["s"], cap=st["cap"], nsb=st["nsb"],
                                                    n_exp=n_exp)

        yp, ys = _experts(streams["p"]["xs"], streams["s"]["xs"], streams["p"]["gate"], streams["s"]["gate"],
                          w_gate, w_up, w_down, l)
        for st, y in ((streams["p"], yp), (streams["s"], ys)):
            st["x"] = _combine(st["pt"], y, st["x1"], mods[l, st["mrow"]], ln2_g[l][None], ln2_b[l][None],
                               alpha=alpha, seq=st["s"], cap=st["cap"],
                               rows_per_mod=st["tiles_per_mod"] * ROW_TILE)

    y_prompt = streams["p"]["x"].reshape(bp, sp, d)
    y_sample = streams["s"]["x"].reshape(bs, ss, d)
    return (y_prompt, y_sample, jnp.stack(outs["ckv"], axis=1), jnp.stack(outs["kpe"], axis=1),
            jnp.stack(outs["k"], axis=1), jnp.stack(outs["v"], axis=1))
```

```python
import functools
import math

import numpy as np
import jax
import jax.numpy as jnp
from jax import lax
from jax.experimental import pallas as pl
from jax.experimental.pallas import tpu as pltpu

f32 = jnp.float32
bf16 = jnp.bfloat16

MLA_HEADS = 8
QK_NOPE = 128
QK_ROPE = 64
V_HEAD = 128
POOL_WINDOWS = (2, 4, 8, 16)
SWA_KV = 4
SWA_HD = 64
WINDOW = 128
CAP_FACTOR = 2
GRID_W = 64
ROPE_BASE = 10000.0
RMS_EPS = 1e-6
LN_EPS = 1e-5

LOG2E = 1.4426950408889634
LANES = 128
ROW_TILE = 256
MIB = 1024 * 1024


def _cparams(sem, vmem_mib):
    return pltpu.CompilerParams(dimension_semantics=sem, vmem_limit_bytes=vmem_mib * MIB)


def _dot(a, b):
    return jnp.dot(a, b, preferred_element_type=f32)


def _dot_nt(a, b):
    return lax.dot_general(a, b, (((1,), (1,)), ((), ())), preferred_element_type=f32)


def _silu(x):
    return x / (1.0 + jnp.exp(-x))


def _rope(r, c, sa, sb):
    return r * c + pltpu.roll(r, LANES - 1, axis=1) * sa + pltpu.roll(r, 1, axis=1) * sb


def _pipelined(n, produce, consume):
    produce(0)
    for i in range(n):
        if i + 1 < n:
            produce(i + 1)
        consume(i)


def _layer_norm(z, g, b):
    mu = jnp.mean(z, axis=-1, keepdims=True)
    zc = z - mu
    var = jnp.mean(zc * zc, axis=-1, keepdims=True)
    return zc * lax.rsqrt(var + LN_EPS) * g + b


def _mod_kernel(c_ref, w_ref, b_ref, o_ref):
    a = _silu(c_ref[...]).astype(bf16)
    o_ref[0] = _dot(a, w_ref[0].astype(bf16)) + b_ref[0]


def _modulation(cond, w_ada, b_ada):
    depth, d, n = w_ada.shape
    r = cond.shape[0]
    tn = 1024
    return pl.pallas_call(
        _mod_kernel,
        out_shape=jax.ShapeDtypeStruct((depth, r, n), f32),
        grid=(depth, n // tn),
        in_specs=[pl.BlockSpec((r, d), lambda l, j: (0, 0)),
                  pl.BlockSpec((1, d, tn), lambda l, j: (l, 0, j)),
                  pl.BlockSpec((1, 1, tn), lambda l, j: (l, 0, j))],
        out_specs=pl.BlockSpec((1, r, tn), lambda l, j: (l, 0, j)),
        compiler_params=_cparams(("arbitrary", "arbitrary"), 40),
        name="adaln_modulation",
    )(cond, w_ada, b_ada.reshape(depth, 1, n))


def _in_even_kernel(x_ref, mod_ref, wina_ref, winb_ref, qn_ref, kvn_ref, wuq_ref, wukv_ref, rc_ref, rsa_ref, rsb_ref,
                    q_ref, ckv_ref, kv_ref, kper_ref, kpea_ref, u_ref, proj_sc, *, rope, q_lora, kv_lora, qscale):
    sh = mod_ref[0, 0:1, :]
    sc = 1.0 + mod_ref[0, 1:2, :]
    o_kv = q_lora
    o_pe = q_lora + kv_lora
    o_u = o_pe + LANES
    sub = proj_sc.shape[1]

    def project(i):
        h = (x_ref[i * sub:(i + 1) * sub, :] * sc + sh).astype(bf16)
        pa = _dot(h, wina_ref[...])
        proj_sc[i % 2, :, 0:o_pe] = pa[:, 0:o_pe]
        kpe = pa[:, o_pe:]
        proj_sc[i % 2, :, o_pe:o_u] = jnp.concatenate([kpe, jnp.zeros((sub, LANES - kpe.shape[1]), f32)], axis=1)
        proj_sc[i % 2, :, o_u:] = _dot(h, winb_ref[...])

    def finish(i):
        rows = slice(i * sub, (i + 1) * sub)
        pq = proj_sc[i % 2, :, 0:o_kv]
        pkv = proj_sc[i % 2, :, o_kv:o_pe]
        kpe = proj_sc[i % 2, :, o_pe:o_u]
        u_ref[rows, :] = proj_sc[i % 2, :, o_u:]
        cq = pq * lax.rsqrt(jnp.mean(pq * pq, axis=-1, keepdims=True) + RMS_EPS) * qn_ref[...]
        ckv = pkv * lax.rsqrt(jnp.mean(pkv * pkv, axis=-1, keepdims=True) + RMS_EPS) * kvn_ref[...]
        ckv_ref[rows, :] = ckv
        kv_ref[rows, :] = _dot(ckv.astype(bf16), wukv_ref[...]).astype(bf16)
        kper_ref[rows, :] = kpe
        q = _dot(cq.astype(bf16), wuq_ref[...]) * qscale
        if rope:
            c, sa, sb = rc_ref[rows, :], rsa_ref[rows, :], rsb_ref[rows, :]
            kpea_ref[rows, :] = _rope(kpe, c, sa, sb).astype(bf16)
        else:
            kpea_ref[rows, :] = kpe.astype(bf16)
        for hd in range(MLA_HEADS):
            a = 2 * hd * LANES
            q_ref[rows, a:a + LANES] = q[:, a:a + LANES].astype(bf16)
            r = q[:, a + LANES:a + 2 * LANES]
            if rope:
                r = _rope(r, c, sa, sb)
            q_ref[rows, a + LANES:a + 2 * LANES] = r.astype(bf16)

    _pipelined(x_ref.shape[0] // sub, project, finish)


def _in_even(x, mods, w_in_a, w_in_b, qn, kvn, wuq, wukv, rtab, *, rope, rows_per_mod, seq):
    t, d = x.shape
    q_lora, kv_lora = qn.shape[1], kvn.shape[1]
    n_u = w_in_b.shape[1]
    n_in = q_lora + kv_lora + LANES + n_u
    nq, nkv = wuq.shape[1], wukv.shape[1]
    tm = ROW_TILE * SUB_TILES
    assert t % tm == 0 and rows_per_mod % tm == 0
    tiles_per_mod = rows_per_mod // tm
    tiles_per_seq = max(1, seq // tm)
    row = lambda i: (i, 0)
    const = lambda i: (0, 0)
    rmap = lambda i: (i % tiles_per_seq, 0)
    kern = functools.partial(_in_even_kernel, rope=rope, q_lora=q_lora, kv_lora=kv_lora,
                             qscale=(QK_NOPE + QK_ROPE) ** -0.5 * LOG2E)
    return pl.pallas_call(
        kern,
        out_shape=(jax.ShapeDtypeStruct((t, nq), bf16), jax.ShapeDtypeStruct((t, kv_lora), f32),
                   jax.ShapeDtypeStruct((t, nkv), bf16), jax.ShapeDtypeStruct((t, LANES), f32),
                   jax.ShapeDtypeStruct((t, LANES), bf16), jax.ShapeDtypeStruct((t, n_u), f32)),
        grid=(t // tm,),
        in_specs=[pl.BlockSpec((tm, d), row),
                  pl.BlockSpec((1, 6, d), lambda i: (i // tiles_per_mod, 0, 0)),
                  pl.BlockSpec(w_in_a.shape, const), pl.BlockSpec(w_in_b.shape, const),
                  pl.BlockSpec((1, q_lora), const), pl.BlockSpec((1, kv_lora), const),
                  pl.BlockSpec((q_lora, nq), const), pl.BlockSpec((kv_lora, nkv), const),
                  pl.BlockSpec((tm, LANES), rmap), pl.BlockSpec((tm, LANES), rmap), pl.BlockSpec((tm, LANES), rmap)],
        out_specs=(pl.BlockSpec((tm, nq), row), pl.BlockSpec((tm, kv_lora), row), pl.BlockSpec((tm, nkv), row),
                   pl.BlockSpec((tm, LANES), row), pl.BlockSpec((tm, LANES), row), pl.BlockSpec((tm, n_u), row)),
        scratch_shapes=[pltpu.VMEM((2, ROW_TILE, n_in), f32)],
        compiler_params=_cparams(("arbitrary",), 56),
        name="in_proj_even",
    )(x, mods, w_in_a, w_in_b, qn, kvn, wuq, wukv, *rtab)


def _mm_kernel(x_ref, w_ref, o_ref):
    o_ref[...] = _dot(x_ref[...].astype(bf16), w_ref[...]).astype(o_ref.dtype)


def _matmul(x, w, out_dtype, tm=512):
    m, k = x.shape
    n = w.shape[1]
    tm = min(tm, m)
    assert m % tm == 0
    return pl.pallas_call(
        _mm_kernel,
        out_shape=jax.ShapeDtypeStruct((m, n), out_dtype),
        grid=(m // tm,),
        in_specs=[pl.BlockSpec((tm, k), lambda i: (i, 0)), pl.BlockSpec((k, n), lambda i: (0, 0))],
        out_specs=pl.BlockSpec((tm, n), lambda i: (i, 0)),
        compiler_params=_cparams(("arbitrary",), 24),
        name="ctx_kv_up",
    )(x, w)


def _pool_kernel(u_ref, wp_ref, ps_ref, o_ref, *, seq):
    gc = wp_ref.shape[1]
    t = lax.broadcasted_iota(jnp.int32, (seq, 1), 0)

    def down(a, k):
        return jnp.where(t >= k, pltpu.roll(a, k, axis=0), 0.0)

    def up(a, k):
        return jnp.where(t < seq - k, pltpu.roll(a, seq - k, axis=0), 0.0)

    for gi, w in enumerate(POOL_WINDOWS):
        half = w // 2
        a = u_ref[:, gi * gc:(gi + 1) * gc]
        trail, lead, k = a, a, 1
        while k < half:
            trail = trail + down(trail, k)
            lead = lead + up(lead, k)
            k *= 2
        win = down(trail, 1) + lead
        cnt = (jnp.minimum(t + half, seq) - jnp.maximum(t - half, 0)).astype(f32)
        pooled = win / cnt - a
        mixed = _dot(pooled.astype(bf16), wp_ref[gi])
        o_ref[:, gi * gc:(gi + 1) * gc] = (mixed * ps_ref[:, gi * gc:(gi + 1) * gc]).astype(bf16)


def _pool(u, w_pool, pool_scale, seq):
    t, ch = u.shape
    g, gc, _ = w_pool.shape
    return pl.pallas_call(
        functools.partial(_pool_kernel, seq=seq),
        out_shape=jax.ShapeDtypeStruct((t, ch), bf16),
        grid=(t // seq,),
        in_specs=[pl.BlockSpec((seq, ch), lambda b: (b, 0)),
                  pl.BlockSpec((g, gc, gc), lambda b: (0, 0, 0)),
                  pl.BlockSpec((1, ch), lambda b: (0, 0))],
        out_specs=pl.BlockSpec((seq, ch), lambda b: (b, 0)),
        compiler_params=_cparams(("arbitrary",), 40),
        name="pool_mixer",
    )(u, w_pool, pool_scale)


def _mla_kernel(*refs, has_ctx):
    if has_ctx:
        q_ref, kv_ref, kpe_ref, kvc_ref, kpec_ref, o_ref, sc_sc = refs
    else:
        q_ref, kv_ref, kpe_ref, o_ref, sc_sc = refs
    seq = kv_ref.shape[0]

    def score(hd):
        a = 2 * hd * LANES
        q = q_ref[:, a:a + 2 * LANES]
        k = jnp.concatenate([kv_ref[:, a:a + LANES], kpe_ref[...]], axis=1)
        sc_sc[hd % 2, :, 0:seq] = _dot_nt(q, k)
        if has_ctx:
            kc = jnp.concatenate([kvc_ref[:, a:a + LANES], kpec_ref[...]], axis=1)
            sc_sc[hd % 2, :, seq:] = _dot_nt(q, kc)

    def attend(hd):
        a = 2 * hd * LANES
        x = sc_sc[hd % 2]
        m = x.max(axis=-1, keepdims=True)
        p = jnp.exp2(x - m)
        l = p.sum(axis=-1, keepdims=True)
        p = p.astype(bf16)
        o = _dot(p[:, 0:seq], kv_ref[:, a + LANES:a + 2 * LANES])
        if has_ctx:
            o = o + _dot(p[:, seq:], kvc_ref[:, a + LANES:a + 2 * LANES])
        o_ref[:, hd * LANES:(hd + 1) * LANES] = (o / l).astype(bf16)

    score(0)
    for hd in range(MLA_HEADS):
        if hd + 1 < MLA_HEADS:
            score(hd + 1)
        attend(hd)


def _mla_attention(q, kv, kpe, ctx, seq):
    t, nq = q.shape
    tq = ROW_TILE
    tps = seq // tq
    in_specs = [pl.BlockSpec((tq, nq), lambda b, j: (b * tps + j, 0)),
                pl.BlockSpec((seq, nq), lambda b, j: (b, 0)),
                pl.BlockSpec((seq, LANES), lambda b, j: (b, 0))]
    args = [q, kv, kpe]
    if ctx is not None:
        kvc, kpec, past = ctx
        in_specs += [pl.BlockSpec((past, nq), lambda b, j: (b, 0)),
                     pl.BlockSpec((past, LANES), lambda b, j: (b, 0))]
        args += [kvc, kpec]
    n_keys = seq + (ctx[2] if ctx is not None else 0)
    return pl.pallas_call(
        functools.partial(_mla_kernel, has_ctx=ctx is not None),
        out_shape=jax.ShapeDtypeStruct((t, MLA_HEADS * V_HEAD), bf16),
        grid=(t // seq, tps),
        in_specs=in_specs,
        out_specs=pl.BlockSpec((tq, MLA_HEADS * V_HEAD), lambda b, j: (b * tps + j, 0)),
        scratch_shapes=[pltpu.VMEM((2, tq, n_keys), f32)],
        compiler_params=_cparams(("arbitrary", "arbitrary"), 48),
        name="mla_attention",
    )(*args)


SUB_TILES = 2
SUB_ROWS = 128


def _out_ln_kernel(*refs, n_pieces, alpha):
    pieces = refs[:n_pieces]
    w_ref, x_ref, mod_ref, g_ref, b_ref, wr_ref, x1_ref, h2_ref, lg_ref, mix_sc = refs[n_pieces:]
    g1 = mod_ref[0, 2:3, :]
    sc2 = 1.0 + mod_ref[0, 4:5, :]
    sh2 = mod_ref[0, 3:4, :]
    sub = mix_sc.shape[1]

    def project(i):
        rows = slice(i * sub, (i + 1) * sub)
        off = 0
        mix = None
        for p in pieces:
            kp = p.shape[1]
            part = _dot(p[rows, :], w_ref[off:off + kp, :])
            mix = part if mix is None else mix + part
            off += kp
        mix_sc[i % 2] = mix

    def finish(i):
        rows = slice(i * sub, (i + 1) * sub)
        x1 = _layer_norm(alpha * x_ref[rows, :] + g1 * mix_sc[i % 2], g_ref[...], b_ref[...])
        x1_ref[rows, :] = x1
        h2 = x1 * sc2 + sh2
        h2h = h2.astype(bf16)
        h2_ref[rows, :] = h2h
        h2l = (h2 - h2h.astype(f32)).astype(bf16)
        hh = _dot(h2h, wr_ref[...])
        lg_ref[rows, :] = hh[:, 0:LANES] + hh[:, LANES:] + _dot(h2l, wr_ref[:, 0:LANES])

    _pipelined(x_ref.shape[0] // sub, project, finish)


def _out_ln(pieces, w_out, x, mods, ln_g, ln_b, wr, *, alpha, rows_per_mod):
    t, d = x.shape
    tm = ROW_TILE * SUB_TILES
    assert t % tm == 0 and rows_per_mod % tm == 0
    row = lambda i: (i, 0)
    const = lambda i: (0, 0)
    in_specs = [pl.BlockSpec((tm, p.shape[1]), row) for p in pieces]
    in_specs += [pl.BlockSpec(w_out.shape, const), pl.BlockSpec((tm, d), row),
                 pl.BlockSpec((1, 6, d), lambda i: (i // (rows_per_mod // tm), 0, 0)),
                 pl.BlockSpec((1, d), const), pl.BlockSpec((1, d), const),
                 pl.BlockSpec((d, 2 * LANES), const)]
    return pl.pallas_call(
        functools.partial(_out_ln_kernel, n_pieces=len(pieces), alpha=alpha),
        out_shape=(jax.ShapeDtypeStruct((t, d), f32), jax.ShapeDtypeStruct((t, d), bf16),
                   jax.ShapeDtypeStruct((t, LANES), f32)),
        grid=(t // tm,),
        in_specs=in_specs,
        out_specs=(pl.BlockSpec((tm, d), row), pl.BlockSpec((tm, d), row), pl.BlockSpec((tm, LANES), row)),
        scratch_shapes=[pltpu.VMEM((2, SUB_ROWS, d), f32)],
        compiler_params=_cparams(("arbitrary",), 56),
        name="out_proj_ln",
    )(*pieces, w_out, x, mods, ln_g, ln_b, wr)


SEARCH_BITS = 3
ROUTE_TOKENS = 1024
DISPATCH_ROWS = 512


def _kth_largest_bits(bits, cap):
    thr = jnp.zeros((bits.shape[0], 1), jnp.int32)
    hi = 31
    while hi > 0:
        lo = max(hi - SEARCH_BITS, 0) if hi % SEARCH_BITS == 0 else hi - hi % SEARCH_BITS
        digit = jnp.zeros_like(thr)
        for c in range(1, 1 << (hi - lo)):
            cnt = jnp.sum(jnp.where(bits >= (thr | (c << lo)), 1.0, 0.0), axis=1, keepdims=True)
            digit = digit + jnp.where(cnt >= cap, 1, 0)
        thr = thr | (digit * (1 << lo))
        hi = lo
    return thr


def _lane_prefix(mask, tri):
    n_rows, n = mask.shape
    blocks = [mask[:, k:k + LANES] for k in range(0, n, LANES)]
    local = _dot(jnp.concatenate(blocks, axis=0).astype(bf16), tri)
    out, carry = [], None
    for i in range(len(blocks)):
        blk = local[i * n_rows:(i + 1) * n_rows, :]
        out.append(blk if carry is None else blk + carry)
        total = blk[:, LANES - 1:LANES]
        carry = total if carry is None else carry + total
    return jnp.concatenate(out, axis=1)


def _route_kernel(lg_ref, h2_ref, tri_ref, xs_ref, gate_ref, pt_ref, aff_sc, pos_sc, *, cap, seq, n_exp, nsb):
    tri = tri_ref[...]
    pos_sc[...] = jnp.full(pos_sc.shape, -1.0, f32)
    for sb in range(nsb):
        lg = lg_ref[sb * seq:(sb + 1) * seq, :].T[0:n_exp, :]
        z = jnp.exp(lg - lg.max(axis=0, keepdims=True))
        aff = z / z.sum(axis=0, keepdims=True)
        bits = pltpu.bitcast(aff, jnp.int32)
        thr = _kth_largest_bits(bits, cap)
        gt = bits > thr
        eq = bits == thr
        need = cap - jnp.sum(jnp.where(gt, 1.0, 0.0), axis=1, keepdims=True)
        eq_rank = _lane_prefix(jnp.where(eq, 1.0, 0.0), tri)
        sel = gt | (eq & (eq_rank <= need))
        sel_rank = _lane_prefix(jnp.where(sel, 1.0, 0.0), tri)
        aff_sc[sb] = aff
        pos_sc[sb, 0:n_exp, :] = jnp.where(sel, sel_rank - 1.0, -1.0)

    slot_col = lax.broadcasted_iota(jnp.int32, (cap, seq), 0).astype(f32)

    eg = max(1, min(n_exp, DISPATCH_ROWS // cap))

    def dispatch(g, carry):
        for sb in range(nsb):
            rows = slice(sb * cap, (sb + 1) * cap)
            onehot = []
            for k in range(eg):
                e = g * eg + k
                hit = pos_sc[sb, pl.ds(e, 1), :] == slot_col
                onehot.append(jnp.where(hit, 1.0, 0.0).astype(bf16))
                gate = jnp.sum(jnp.where(hit, aff_sc[sb, pl.ds(e, 1), :], 0.0), axis=1, keepdims=True)
                gate_ref[e, rows, :] = jnp.broadcast_to(gate, (cap, LANES))
            picked = _dot(jnp.concatenate(onehot, axis=0), h2_ref[sb * seq:(sb + 1) * seq, :])
            for k in range(eg):
                xs_ref[g * eg + k, rows, :] = picked[k * cap:(k + 1) * cap, :].astype(bf16)
        return carry

    lax.fori_loop(0, n_exp // eg, dispatch, 0)

    lane = lax.broadcasted_iota(jnp.int32, (1, LANES), 1)
    for sb in range(nsb):
        pos_t = pos_sc[sb].T
        for j in range(n_exp * cap // LANES):
            if cap >= LANES:
                e = j * LANES // cap
                target = (lane + (j * LANES - e * cap)).astype(f32)
                hit = pos_t[:, e:e + 1] == target
            else:
                per = LANES // cap
                hit = None
                for k in range(per):
                    e = j * per + k
                    target = jnp.where((lane >= k * cap) & (lane < (k + 1) * cap), lane - k * cap, -7).astype(f32)
                    m = pos_t[:, e:e + 1] == target
                    hit = m if hit is None else hit | m
            pt_ref[sb * seq:(sb + 1) * seq, j * LANES:(j + 1) * LANES] = jnp.where(hit, 1.0, 0.0).astype(bf16)


def _route(lg, h2, tri, *, seq, cap, nsb, n_exp):
    t, d = h2.shape
    nb = t // seq
    assert nb % nsb == 0
    return pl.pallas_call(
        functools.partial(_route_kernel, cap=cap, seq=seq, n_exp=n_exp, nsb=nsb),
        out_shape=(jax.ShapeDtypeStruct((n_exp, nb * cap, d), bf16),
                   jax.ShapeDtypeStruct((n_exp, nb * cap, LANES), f32),
                   jax.ShapeDtypeStruct((t, n_exp * cap), bf16)),
        grid=(nb // nsb,),
        in_specs=[pl.BlockSpec((nsb * seq, LANES), lambda b: (b, 0)),
                  pl.BlockSpec((nsb * seq, d), lambda b: (b, 0)),
                  pl.BlockSpec((LANES, LANES), lambda b: (0, 0))],
        out_specs=(pl.BlockSpec((n_exp, nsb * cap, d), lambda b: (0, b, 0)),
                   pl.BlockSpec((n_exp, nsb * cap, LANES), lambda b: (0, b, 0)),
                   pl.BlockSpec((nsb * seq, n_exp * cap), lambda b: (b, 0))),
        scratch_shapes=[pltpu.VMEM((nsb, n_exp, seq), f32), pltpu.VMEM((nsb, LANES, seq), f32)],
        compiler_params=_cparams(("arbitrary",), 56),
        name="route_dispatch",
    )(lg, h2, tri)


EXPERT_TILE = 512
EXPERT_ROWS = 256


def _expert_kernel(xp_ref, xs_ref, gp_ref, gs_ref, wg_ref, wu_ref, wd_ref, yp_ref, ys_ref,
                   hid_sc, wa_sc, wb_sc, wc_sc, *, nf, tf):
    j = pl.program_id(1)
    mp, ms = xp_ref.shape[1], xs_ref.shape[1]
    cp, cs = min(EXPERT_ROWS, mp), min(EXPERT_ROWS, ms)
    chunks = [(xp_ref, gp_ref, yp_ref, r, r, cp) for r in range(0, mp, cp)]
    chunks += [(xs_ref, gs_ref, ys_ref, r, mp + r, cs) for r in range(0, ms, cs)]

    @pl.when(j < nf)
    def _():
        wa_sc[...] = wg_ref[0, 0].astype(bf16)
        wb_sc[...] = wu_ref[0, 0].astype(bf16)
        for x_ref, _, _, r, m, n in chunks:
            xc = x_ref[0, r:r + n, :]
            g = _dot(xc, wa_sc[...])
            u = _dot(xc, wb_sc[...])
            hid_sc[j, m:m + n, :] = (_silu(g) * u).astype(bf16)

    @pl.when(j >= nf)
    def _():
        wc_sc[...] = wd_ref[0, 0].astype(bf16)
        for _, g_ref, y_ref, r, m, n in chunks:
            acc = _dot(hid_sc[0, m:m + n, :], wc_sc[0:tf, :])
            for f in range(1, nf):
                acc = acc + _dot(hid_sc[f, m:m + n, :], wc_sc[f * tf:(f + 1) * tf, :])
            y_ref[0, r:r + n, :] = (acc * g_ref[0, r:r + n, 0:1]).astype(bf16)


def _experts(xp, xs, gp, gs, w_gate, w_up, w_down, layer):
    n_exp, mp, d = xp.shape
    ms = xs.shape[1]
    ff = w_gate.shape[3]
    tf = tn = EXPERT_TILE
    nf, nn = ff // tf, d // tn
    x_map = lambda e, j: (e, 0, 0)
    xin_map = lambda e, j: (jnp.minimum(e + (j >= nf).astype(jnp.int32), n_exp - 1), 0, 0)
    up_map = lambda e, j: (layer, e, 0, jnp.minimum(j, nf - 1))
    dn_map = lambda e, j: (layer, e, 0, jnp.maximum(j - nf, 0))
    y_map = lambda e, j: (e, 0, jnp.maximum(j - nf, 0))
    return pl.pallas_call(
        functools.partial(_expert_kernel, nf=nf, tf=tf),
        out_shape=(jax.ShapeDtypeStruct((n_exp, mp, d), bf16), jax.ShapeDtypeStruct((n_exp, ms, d), bf16)),
        grid=(n_exp, nf + nn),
        in_specs=[pl.BlockSpec((1, mp, d), xin_map), pl.BlockSpec((1, ms, d), xin_map),
                  pl.BlockSpec((1, mp, LANES), x_map), pl.BlockSpec((1, ms, LANES), x_map),
                  pl.BlockSpec((1, 1, d, tf), up_map), pl.BlockSpec((1, 1, d, tf), up_map),
                  pl.BlockSpec((1, 1, ff, tn), dn_map)],
        out_specs=(pl.BlockSpec((1, mp, tn), y_map), pl.BlockSpec((1, ms, tn), y_map)),
        scratch_shapes=[pltpu.VMEM((nf, mp + ms, tf), bf16), pltpu.VMEM((d, tf), bf16),
                        pltpu.VMEM((d, tf), bf16), pltpu.VMEM((ff, tn), bf16)],
        compiler_params=_cparams(("arbitrary", "arbitrary"), 58),
        name="expert_swiglu",
    )(xp, xs, gp, gs, w_gate, w_up, w_down)


def _combine_kernel(pt_ref, y_ref, x_ref, mod_ref, g_ref, b_ref, o_ref, ffn_sc, *, alpha, seq, cap):
    n_exp, _, d = y_ref.shape
    g2 = mod_ref[0, 5:6, :]
    sub = ffn_sc.shape[1]

    def scatter(i):
        sq = i * sub // seq
        y = y_ref[:, sq * cap:(sq + 1) * cap, :].reshape(n_exp * cap, d)
        ffn_sc[i % 2] = _dot(pt_ref[i * sub:(i + 1) * sub, :], y)

    def finish(i):
        rows = slice(i * sub, (i + 1) * sub)
        o_ref[rows, :] = _layer_norm(alpha * x_ref[rows, :] + g2 * ffn_sc[i % 2], g_ref[...], b_ref[...])

    _pipelined(x_ref.shape[0] // sub, scatter, finish)


def _combine(pt, y, x1, mods, ln_g, ln_b, *, alpha, seq, cap, rows_per_mod):
    t, d = x1.shape
    n_exp = y.shape[0]
    tm = ROW_TILE * SUB_TILES
    assert t % tm == 0 and rows_per_mod % tm == 0 and (tm % seq == 0 or seq % tm == 0)
    spt = max(1, tm // seq)
    tps = max(1, seq // tm)
    row = lambda i: (i, 0)
    const = lambda i: (0, 0)
    return pl.pallas_call(
        functools.partial(_combine_kernel, alpha=alpha, seq=seq, cap=cap),
        out_shape=jax.ShapeDtypeStruct((t, d), f32),
        grid=(t // tm,),
        in_specs=[pl.BlockSpec((tm, n_exp * cap), row),
                  pl.BlockSpec((n_exp, spt * cap, d), lambda i: (0, i // tps, 0)),
                  pl.BlockSpec((tm, d), row),
                  pl.BlockSpec((1, 6, d), lambda i: (i // (rows_per_mod // tm), 0, 0)),
                  pl.BlockSpec((1, d), const), pl.BlockSpec((1, d), const)],
        out_specs=pl.BlockSpec((tm, d), row),
        scratch_shapes=[pltpu.VMEM((2, ROW_TILE, d), f32)],
        compiler_params=_cparams(("arbitrary",), 56),
        name="combine_ln",
    )(pt, y, x1, mods, ln_g, ln_b)


def _pad_pairs(chunk, lane):
    swapped = pltpu.roll(chunk, LANES // 2, axis=1)
    lo = lane < LANES // 2
    return [jnp.where(lo, chunk, 0.0), jnp.where(lo, 0.0, swapped),
            jnp.where(lo, swapped, 0.0), jnp.where(lo, 0.0, chunk)]


def _in_odd_kernel(x_ref, mod_ref, win_ref, rc_ref, rsa_ref, rsb_ref,
                   q_ref, ka_ref, va_ref, kf_ref, vf_ref, proj_sc, *, rope, nq, nkv, qscale, tqa):
    sh = mod_ref[0, 0:1, :]
    sc = 1.0 + mod_ref[0, 1:2, :]
    pairs = nq // LANES // SWA_KV
    lane = lax.broadcasted_iota(jnp.int32, (1, LANES), 1)
    sub = proj_sc.shape[1]
    piece = min(sub, tqa)

    def project(i):
        h = (x_ref[i * sub:(i + 1) * sub, :] * sc + sh).astype(bf16)
        proj_sc[i % 2] = _dot(h, win_ref[...])

    def finish(i):
        rows = slice(i * sub, (i + 1) * sub)
        k = proj_sc[i % 2, :, nq:nq + nkv]
        v = proj_sc[i % 2, :, nq + nkv:nq + 2 * nkv]
        kf_ref[rows, :] = k
        vf_ref[rows, :] = v
        if rope:
            c, sa, sb = rc_ref[rows, :], rsa_ref[rows, :], rsb_ref[rows, :]
        for ci in range(nq // LANES):
            r = proj_sc[i % 2, :, ci * LANES:(ci + 1) * LANES]
            if rope:
                r = _rope(r, c, sa, sb)
            r = (r * qscale).astype(bf16)
            hk, pr = ci // pairs, ci % pairs
            for p0 in range(0, sub, piece):
                tok = i * sub + p0
                q_ref[tok // tqa, hk, pr * tqa + tok % tqa:pr * tqa + tok % tqa + piece, :] = r[p0:p0 + piece, :]
        for a in range(0, nkv, LANES):
            kc = k[:, a:a + LANES]
            if rope:
                kc = _rope(kc, c, sa, sb)
            for j, (kp, vp) in enumerate(zip(_pad_pairs(kc, lane), _pad_pairs(v[:, a:a + LANES], lane))):
                o = 4 * a + j * LANES
                ka_ref[rows, o:o + LANES] = kp.astype(bf16)
                va_ref[rows, o:o + LANES] = vp.astype(bf16)

    _pipelined(x_ref.shape[0] // sub, project, finish)


def _in_odd(x, mods, w_in, rtab, *, rope, tqa, rows_per_mod, seq):
    t, d = x.shape
    n_in = w_in.shape[1]
    nkv = SWA_KV * SWA_HD
    nq = n_in - 2 * nkv
    tm = ROW_TILE * SUB_TILES
    assert tm % tqa == 0 and ROW_TILE % tqa == 0 and t % tm == 0 and rows_per_mod % tm == 0
    tiles_per_mod = rows_per_mod // tm
    tiles_per_seq = max(1, seq // tm)
    q_rows = nq // LANES // SWA_KV * tqa
    row = lambda i: (i, 0)
    rmap = lambda i: (i % tiles_per_seq, 0)
    kern = functools.partial(_in_odd_kernel, rope=rope, nq=nq, nkv=nkv, qscale=SWA_HD ** -0.5 * LOG2E, tqa=tqa)
    return pl.pallas_call(
        kern,
        out_shape=(jax.ShapeDtypeStruct((t // tqa, SWA_KV, q_rows, LANES), bf16),
                   jax.ShapeDtypeStruct((t, 4 * nkv), bf16),
                   jax.ShapeDtypeStruct((t, 4 * nkv), bf16), jax.ShapeDtypeStruct((t, nkv), f32),
                   jax.ShapeDtypeStruct((t, nkv), f32)),
        grid=(t // tm,),
        in_specs=[pl.BlockSpec((tm, d), row),
                  pl.BlockSpec((1, 6, d), lambda i: (i // tiles_per_mod, 0, 0)),
                  pl.BlockSpec((d, n_in), lambda i: (0, 0)),
                  pl.BlockSpec((tm, LANES), rmap), pl.BlockSpec((tm, LANES), rmap), pl.BlockSpec((tm, LANES), rmap)],
        out_specs=(pl.BlockSpec((tm // tqa, SWA_KV, q_rows, LANES), lambda i: (i, 0, 0, 0)),
                   pl.BlockSpec((tm, 4 * nkv), row), pl.BlockSpec((tm, 4 * nkv), row),
                   pl.BlockSpec((tm, nkv), row), pl.BlockSpec((tm, nkv), row)),
        scratch_shapes=[pltpu.VMEM((2, ROW_TILE, n_in), f32)],
        compiler_params=_cparams(("arbitrary",), 56),
        name="in_proj_odd",
    )(x, mods, w_in, *rtab)


GQA_BLOCKS = 4


def _gqa_kernel(*refs, windowed, seq, tq):
    if windowed:
        q_ref, ka_ref, va_ref, kcc_ref, vcc_ref, sink_ref, o_ref, sc_sc, kc_ref, vc_ref = refs

        @pl.when(pl.program_id(1) == 0)
        def _():
            lane = lax.broadcasted_iota(jnp.int32, (1, LANES), 1)
            for a in range(0, kcc_ref.shape[1], LANES):
                kparts = _pad_pairs(kcc_ref[:, a:a + LANES].astype(f32), lane)
                vparts = _pad_pairs(vcc_ref[:, a:a + LANES].astype(f32), lane)
                for i in range(4):
                    o = 4 * a + i * LANES
                    kc_ref[:, o:o + LANES] = kparts[i].astype(bf16)
                    vc_ref[:, o:o + LANES] = vparts[i].astype(bf16)
    else:
        q_ref, ka_ref, va_ref, sink_ref, o_ref, sc_sc = refs
    nblk = q_ref.shape[0]
    pairs = q_ref.shape[2] // tq
    rows, bias = [], []
    for blk in range(nblk):
        if windowed:
            span = tq + 2 * WINDOW
            q0 = (pl.program_id(1) * nblk + blk) * tq
            start = pl.multiple_of(jnp.clip(q0 - WINDOW, 0, seq - span), LANES)
            kpos = start + lax.broadcasted_iota(jnp.int32, (tq, span), 1)
            qpos = q0 + lax.broadcasted_iota(jnp.int32, (tq, span), 0)
            bias1 = jnp.where(jnp.abs(kpos - qpos) <= WINDOW, 0.0, -1e30)
            bias.append(jnp.concatenate([bias1] * pairs, axis=0))
            rows.append(pl.ds(start, span))
        else:
            rows.append(slice(blk * seq, (blk + 1) * seq))
    n_loc = tq + 2 * WINDOW if windowed else seq
    n_units = 2 * SWA_KV

    def chunk(u):
        return slice(u * LANES, (u + 1) * LANES)

    def score(k):
        blk, u = divmod(k, n_units)
        q = q_ref[blk, u // 2]
        sc = _dot_nt(q, ka_ref[rows[blk], chunk(u)])
        if windowed:
            sc_sc[k % 2, :, 0:n_loc] = sc + bias[blk]
            sc_sc[k % 2, :, n_loc:] = _dot_nt(q, kc_ref[:, chunk(u)])
        else:
            sc_sc[k % 2] = sc

    halves = {}

    def attend(k):
        blk, u = divmod(k, n_units)
        heads = [2 * (u // 2 * pairs + pr) + u % 2 for pr in range(pairs)]
        sink = jnp.concatenate([jnp.broadcast_to(sink_ref[h:h + 1, 0:1], (tq, 1)) for h in heads], axis=0) * LOG2E
        x = sc_sc[k % 2]
        m = jnp.maximum(sink, x.max(axis=-1, keepdims=True))
        p = jnp.exp2(x - m)
        l = jnp.exp2(sink - m) + p.sum(axis=-1, keepdims=True)
        p = p.astype(bf16)
        o = _dot(p[:, 0:n_loc], va_ref[rows[blk], chunk(u)])
        if windowed:
            o = o + _dot(p[:, n_loc:], vc_ref[:, chunk(u)])
        o = o / l
        if u % 2 == 0:
            halves[blk] = o
        else:
            out = halves.pop(blk) + o
            for pr in range(pairs):
                col = (u // 2 * pairs + pr) * LANES
                o_ref[blk * tq:(blk + 1) * tq, col:col + LANES] = out[pr * tq:(pr + 1) * tq, :].astype(bf16)

    _pipelined(nblk * n_units, score, attend)


def _gqa_attention(q, ka, va, sink, ctx, *, seq, windowed):
    t, nk = ka.shape
    tq = WINDOW if windowed else seq
    tps = seq // tq
    nb = t // seq
    q_rows = q.shape[2]
    nq = q_rows // tq * SWA_KV * LANES
    nblk = 1 if windowed else math.gcd(GQA_BLOCKS, nb)
    seqs = 1 if windowed else nblk
    steps = tps // nblk if windowed else 1
    in_specs = [pl.BlockSpec((nblk, SWA_KV, q_rows, LANES), lambda b, j: (b * steps + j, 0, 0, 0)),
                pl.BlockSpec((seqs * seq, nk), lambda b, j: (b, 0)),
                pl.BlockSpec((seqs * seq, nk), lambda b, j: (b, 0))]
    args = [q, ka, va]
    scratch = []
    if windowed:
        kc, vc, past = ctx
        in_specs += [pl.BlockSpec((past, kc.shape[1]), lambda b, j: (b, 0)),
                     pl.BlockSpec((past, vc.shape[1]), lambda b, j: (b, 0))]
        args += [kc, vc]
        scratch = [pltpu.VMEM((past, nk), bf16), pltpu.VMEM((past, nk), bf16)]
    in_specs.append(pl.BlockSpec(sink.shape, lambda b, j: (0, 0)))
    args.append(sink)
    n_keys = tq + 2 * WINDOW + ctx[2] if windowed else seq
    return pl.pallas_call(
        functools.partial(_gqa_kernel, windowed=windowed, seq=seq, tq=tq),
        out_shape=jax.ShapeDtypeStruct((t, nq), bf16),
        grid=(nb // seqs, steps),
        in_specs=in_specs,
        out_specs=pl.BlockSpec((nblk * tq, nq), lambda b, j: (b * steps + j, 0)),
        scratch_shapes=[pltpu.VMEM((2, q_rows, n_keys), f32)] + scratch,
        compiler_params=_cparams(("arbitrary", "arbitrary"), 56),
        name="gqa_attention",
    )(*args)


def _rope_tables(n_tokens, rot_dim):
    rows = n_tokens // GRID_W
    row = jnp.repeat(jnp.arange(rows, dtype=f32), GRID_W)
    col = jnp.tile(jnp.arange(GRID_W, dtype=f32), rows)
    n_freq = rot_dim // 4
    inv = ROPE_BASE ** (-jnp.arange(n_freq, dtype=f32) / n_freq)
    ang = jnp.concatenate([row[:, None] * inv, col[:, None] * inv], -1)
    cos = jnp.repeat(jnp.cos(ang), 2, axis=1)
    sin = jnp.repeat(jnp.sin(ang), 2, axis=1)
    even = (jnp.arange(rot_dim) % 2 == 0)[None, :]
    sa = jnp.where(even, -sin, 0.0)
    sb = jnp.where(even, 0.0, sin)
    reps = LANES // rot_dim
    if reps >= 1:
        return tuple(jnp.tile(a, (1, reps)) for a in (cos, sa, sb))
    raise NotImplementedError


def _split_hi_lo(w):
    hi = w.astype(bf16)
    return hi, (w - hi.astype(f32)).astype(bf16)


def kernel(x_prompt, x_sample, cache_mla_ckv, cache_mla_kpe, cache_swa_k, cache_swa_v, c, c_ctx, w_ada, b_ada, w_in_even, q_norm, kv_norm, w_uq, w_ukv, w_pool, pool_scale, w_out_even, w_in_odd, sink, w_out_odd, ln1_g, ln1_b, w_router, w_gate, w_up, w_down, ln2_g, ln2_b):
    depth, d = w_ada.shape[0], w_ada.shape[1]
    bp, sp, _ = x_prompt.shape
    bs, ss, _ = x_sample.shape
    past = cache_mla_ckv.shape[2]
    n_exp = w_router.shape[2]
    alpha = (2 * depth) ** 0.25
    q_lora, kv_lora = q_norm.shape[1], kv_norm.shape[1]
    pool_ch = pool_scale.shape[1]
    assert sp % ROW_TILE == 0 and ss % ROW_TILE == 0 and ss % GRID_W == 0
    assert QK_ROPE <= LANES and LANES % SWA_HD == 0 and 2 * SWA_HD == LANES

    n_rows = -(-(1 + bs) // 16) * 16
    cond = jnp.zeros((n_rows, d), f32).at[0].set(c_ctx).at[1:1 + bs].set(c)
    mods = _modulation(cond, w_ada, b_ada)
    mods = mods.reshape(depth, n_rows, 6, d)

    streams = {
        "p": dict(x=x_prompt.reshape(bp * sp, d), b=bp, s=sp, rope=False, mrow=slice(0, 1),
                  tiles_per_mod=bp * sp // ROW_TILE),
        "s": dict(x=x_sample.reshape(bs * ss, d), b=bs, s=ss, rope=True, mrow=slice(1, 1 + bs),
                  tiles_per_mod=ss // ROW_TILE),
    }
    for st in streams.values():
        st["cap"] = max(1, CAP_FACTOR * st["s"] // n_exp)
        st["nsb"] = math.gcd(st["b"], max(1, ROUTE_TOKENS // st["s"]))
        assert (n_exp * st["cap"]) % LANES == 0 and (LANES % st["cap"] == 0 or st["cap"] % LANES == 0)
        m_rows = st["b"] * st["cap"]
        assert m_rows % min(EXPERT_ROWS, m_rows) == 0 and m_rows % 16 == 0

    tri = jnp.triu(jnp.ones((LANES, LANES), bf16))
    rope_mla = _rope_tables(ss, QK_ROPE)
    rope_swa = _rope_tables(ss, SWA_HD)
    ident = (jnp.ones((ROW_TILE * SUB_TILES, LANES), f32),) + (jnp.zeros((ROW_TILE * SUB_TILES, LANES), f32),) * 2

    outs = {"ckv": [], "kpe": [], "k": [], "v": []}
    for l in range(depth):
        wr = jnp.concatenate(_split_hi_lo(jnp.pad(w_router[l], ((0, 0), (0, LANES - n_exp)))), axis=1)
        if l % 2 == 0:
            e = l // 2
            o_pool = q_lora + kv_lora + QK_ROPE
            w_in_a = w_in_even[e, :, :o_pool].astype(bf16)
            w_in_b = w_in_even[e, :, o_pool:].astype(bf16)
            wq = w_uq[e].reshape(q_lora, MLA_HEADS, QK_NOPE + QK_ROPE)
            wq = jnp.pad(wq, ((0, 0), (0, 0), (0, 2 * LANES - QK_NOPE - QK_ROPE)))
            wq = wq.reshape(q_lora, MLA_HEADS * 2 * LANES).astype(bf16)
            wkv = w_ukv[e].astype(bf16)
            w_out = w_out_even[e].astype(bf16)
            wp = w_pool[e].astype(bf16)
            kvc = _matmul(cache_mla_ckv[:, e].reshape(bs * past, kv_lora), wkv, bf16)
            kpec = jnp.pad(cache_mla_kpe[:, e].reshape(bs * past, QK_ROPE),
                           ((0, 0), (0, LANES - QK_ROPE))).astype(bf16)
        else:
            o = l // 2
            w_in = w_in_odd[o].astype(bf16)
            w_out = w_out_odd[o].astype(bf16)
            sink_b = jnp.broadcast_to(sink[o][:, None], (sink.shape[1], LANES))
            kc = cache_swa_k[:, o].reshape(bs * past, SWA_KV * SWA_HD).astype(bf16)
            vc = cache_swa_v[:, o].reshape(bs * past, SWA_KV * SWA_HD).astype(bf16)

        for name, st in streams.items():
            m = mods[l, st["mrow"]]
            common = dict(rope=st["rope"], rows_per_mod=st["tiles_per_mod"] * ROW_TILE, seq=st["s"])
            if l % 2 == 0:
                q, ckv, kv, kpe_raw, kpe_att, u = _in_even(
                    st["x"], m, w_in_a, w_in_b, q_norm[e][None], kv_norm[e][None], wq, wkv,
                    rope_mla if st["rope"] else ident, **common)
                pooled = _pool(u, wp, pool_scale[e][None], st["s"])
                ctx = (kvc, kpec, past) if st["rope"] else None
                attn = _mla_attention(q, kv, kpe_att, ctx, st["s"])
                pieces = [attn, pooled]
                if not st["rope"]:
                    outs["ckv"].append(ckv.reshape(st["b"], st["s"], kv_lora))
                    outs["kpe"].append(kpe_raw[:, :QK_ROPE].reshape(st["b"], st["s"], QK_ROPE))
            else:
                q, ka, va, kf, vf = _in_odd(st["x"], m, w_in, rope_swa if st["rope"] else ident,
                                            tqa=WINDOW if st["rope"] else st["s"], **common)
                ctx = (kc, vc, past) if st["rope"] else None
                attn = _gqa_attention(q, ka, va, sink_b, ctx, seq=st["s"], windowed=st["rope"])
                pieces = [attn]
                if not st["rope"]:
                    outs["k"].append(kf.reshape(st["b"], st["s"], SWA_KV, SWA_HD))
                    outs["v"].append(vf.reshape(st["b"], st["s"], SWA_KV, SWA_HD))
            st["x1"], h2, lg = _out_ln(pieces, w_out, st["x"], m, ln1_g[l][None], ln1_b[l][None],
                                       wr, alpha=alpha, rows_per_mod=st["tiles_per_mod"] * ROW_TILE)
            st["xs"], st["gate"], st["pt"] = _route(lg, h2, tri, seq=st["s"], cap=st["cap"], nsb=st["nsb"],
                                                    n_exp=n_exp)

        yp, ys = _experts(streams["p"]["xs"], streams["s"]["xs"], streams["p"]["gate"], streams["s"]["gate"],
                          w_gate, w_up, w_down, l)
        for st, y in ((streams["p"], yp), (streams["s"], ys)):
            st["x"] = _combine(st["pt"], y, st["x1"], mods[l, st["mrow"]], ln2_g[l][None], ln2_b[l][None],
                               alpha=alpha, seq=st["s"], cap=st["cap"],
                               rows_per_mod=st["tiles_per_mod"] * ROW_TILE)

    y_prompt = streams["p"]["x"].reshape(bp, sp, d)
    y_sample = streams["s"]["x"].reshape(bs, ss, d)
    return (y_prompt, y_sample, jnp.stack(outs["ckv"], axis=1), jnp.stack(outs["kpe"], axis=1),
            jnp.stack(outs["k"], axis=1), jnp.stack(outs["v"], axis=1))
```
